```python
import math
import jax, jax.numpy as jnp
from jax import lax
import numpy as np

D_MODEL = 2048
BATCH = 4
SEQ = 2048
DEPTH = 4
DEC_BATCH = 128
DEC_SEQ = 8
PAST_LEN = 8192
PAGE_SIZE = 128

GROUP_WIDTH = D_MODEL // 4
MIX_WIDTH = 4 * GROUP_WIDTH
MLA_HEADS = 4
MLA_NOPE = 128
MLA_ROPE = 64
MLA_V = GROUP_WIDTH // MLA_HEADS
Q_LORA = D_MODEL // 4
KV_LORA = D_MODEL // 8
MLA_SCALE = (MLA_NOPE + MLA_ROPE) ** -0.5
ROPE_THETA = 10000.0
Q_BLOCK = 128
SG_HEADS = 4
SG_HEAD_DIM = GROUP_WIDTH // SG_HEADS
CHUNK = 128
POOL_WINDOWS = (2, 4, 8, 16)
POOL_GROUPS = len(POOL_WINDOWS)
POOL_GROUP = GROUP_WIDTH // POOL_GROUPS
POOL_STATE = max(POOL_WINDOWS) - 1
CONV_WIDTH = 31
CONV_STATE = CONV_WIDTH - 1
MEM_LEN = 256
MEM_HEADS = 4
MEM_HEAD_DIM = 128
MEM_WIDTH = MEM_HEADS * MEM_HEAD_DIM
D_FF = 5632
EPS = 1e-6
NEG = -1e30
A_COLS = Q_LORA + KV_LORA + MLA_ROPE
B_COLS = 2 * GROUP_WIDTH
C_COLS = GROUP_WIDTH
D_COLS = 2 * GROUP_WIDTH
IN_COLS = A_COLS + B_COLS + C_COLS + D_COLS

kernel_name = 'hybrid_mla_gmlp_pool_conv_decoder'


def _rms(x, g):
    xf = x.astype(jnp.float32)
    y = xf * lax.rsqrt(jnp.mean(xf * xf, axis=-1, keepdims=True) + EPS)
    return (y * g.astype(jnp.float32)).astype(x.dtype)


def _ln(x, g, b):
    xf = x.astype(jnp.float32)
    mu = jnp.mean(xf, axis=-1, keepdims=True)
    xc = xf - mu
    y = xc * lax.rsqrt(jnp.mean(xc * xc, axis=-1, keepdims=True) + EPS)
    return (y * g.astype(jnp.float32) + b.astype(jnp.float32)).astype(x.dtype)


def _rope(x, pos):
    half = x.shape[-1] // 2
    inv = ROPE_THETA ** (-jnp.arange(half, dtype=jnp.float32) / half)
    ang = pos.astype(jnp.float32)[:, None] * inv[None, :]
    shape = (x.shape[1],) + (1,) * (x.ndim - 3) + (half,)
    c = jnp.cos(ang).reshape(shape)
    s = jnp.sin(ang).reshape(shape)
    xf = x.astype(jnp.float32)
    x1, x2 = xf[..., :half], xf[..., half:]
    return jnp.concatenate([x1 * c - x2 * s, x2 * c + x1 * s], axis=-1).astype(x.dtype)


def _swiglu(h, w_in, w_out):
    g, u = jnp.split(h @ w_in, 2, axis=-1)
    return (jax.nn.silu(g) * u) @ w_out


def _mla_project(h_a, pos, norm_q, w_qb, norm_kv, w_uk):
    q_c, c_kv, k_r = jnp.split(h_a, [Q_LORA, Q_LORA + KV_LORA], axis=-1)
    q = jnp.einsum('nlc,chd->nlhd', _rms(q_c, norm_q), w_qb)
    q_nope, q_rope = q[..., :MLA_NOPE], q[..., MLA_NOPE:]
    q_lat = jnp.einsum('nlhd,chd->nlhc', q_nope, w_uk)
    return q_lat, _rope(q_rope, pos), _rms(c_kv, norm_kv), _rope(k_r, pos)


def _mla_logits(q_lat, q_rope, c_kv, k_r):
    s = jnp.einsum('nqhc,nkc->nhqk', q_lat, c_kv, preferred_element_type=jnp.float32)
    s = s + jnp.einsum('nqhr,nkr->nhqk', q_rope, k_r, preferred_element_type=jnp.float32)
    return s * MLA_SCALE


def _mla_prompt(q_lat, q_rope, c_kv, k_r):
    n, L = q_lat.shape[:2]
    k_pos = jnp.arange(L)

    def block(i):
        qs = lax.dynamic_slice_in_dim(q_lat, i * Q_BLOCK, Q_BLOCK, axis=1)
        qr = lax.dynamic_slice_in_dim(q_rope, i * Q_BLOCK, Q_BLOCK, axis=1)
        q_pos = i * Q_BLOCK + jnp.arange(Q_BLOCK)
        s = jnp.where(k_pos[None, :] <= q_pos[:, None], _mla_logits(qs, qr, c_kv, k_r), NEG)
        p = jax.nn.softmax(s, axis=-1).astype(c_kv.dtype)
        return jnp.einsum('nhqk,nkc->nqhc', p, c_kv)

    o = lax.map(block, jnp.arange(L // Q_BLOCK))
    return jnp.moveaxis(o, 0, 1).reshape(n, L, MLA_HEADS, KV_LORA)


def _mla_sample(q_lat, q_rope, c_kv_new, k_r_new, c_kv_past, k_r_past):
    t = q_lat.shape[1]
    past = c_kv_past.shape[1]
    s_past = _mla_logits(q_lat, q_rope, c_kv_past, k_r_past)
    s_new = jnp.where(jnp.tril(jnp.ones((t, t), bool)), _mla_logits(q_lat, q_rope, c_kv_new, k_r_new), NEG)
    p = jax.nn.softmax(jnp.concatenate([s_past, s_new], axis=-1), axis=-1).astype(c_kv_new.dtype)
    return (jnp.einsum('nhqk,nkc->nqhc', p[..., :past], c_kv_past)
            + jnp.einsum('nhqk,nkc->nqhc', p[..., past:], c_kv_new))


def _spatial_gate(h_b, w_s, b_s, ln_g, ln_b):
    u, v = jnp.split(jax.nn.gelu(h_b), 2, axis=-1)
    v = _ln(v, ln_g, ln_b)
    n, L, _ = v.shape
    lc = min(L, CHUNK)
    w = jnp.where(jnp.tril(jnp.ones((lc, lc), bool)), w_s[:, :lc, :lc], 0.0)
    vc = v.reshape(n, L // lc, lc, SG_HEADS, SG_HEAD_DIM)
    g = jnp.einsum('hij,ncjhd->ncihd', w, vc) + jnp.transpose(b_s[:, :lc])[None, None, :, :, None]
    return u * g.reshape(n, L, GROUP_WIDTH), v


def _pool_mix(h_c, prefix, pos, w_pool, scale):
    n, L, _ = h_c.shape
    x_ext = jnp.concatenate([prefix, h_c], axis=1)
    cs = jnp.pad(jnp.cumsum(x_ext.astype(jnp.float32), axis=1), ((0, 0), (1, 0), (0, 0)))
    end = cs[:, POOL_STATE + 1:]
    outs = []
    for gi, w in enumerate(POOL_WINDOWS):
        sl = slice(gi * POOL_GROUP, (gi + 1) * POOL_GROUP)
        begin = cs[:, POOL_STATE + 1 - w:POOL_STATE + 1 - w + L, sl]
        cnt = jnp.minimum(pos + 1, w).astype(jnp.float32)[None, :, None]
        outs.append((end[..., sl] - begin) / cnt)
    pooled = jnp.concatenate(outs, axis=-1).astype(h_c.dtype) - h_c
    y = jnp.einsum('nlgc,gcd->nlgd', pooled.reshape(n, L, POOL_GROUPS, POOL_GROUP), w_pool)
    return y.reshape(n, L, GROUP_WIDTH) * scale, x_ext[:, -POOL_STATE:]


def _conv_module(h_d, prefix, w_dw, b_dw, ln_g, ln_b):
    a, g = jnp.split(h_d, 2, axis=-1)
    z = a * jax.nn.sigmoid(g)
    z_ext = jnp.concatenate([prefix, z], axis=1)
    y = lax.conv_general_dilated(z_ext, w_dw[:, None, :], window_strides=(1,), padding='VALID',
                                 dimension_numbers=('NWC', 'WIO', 'NWC'),
                                 feature_group_count=GROUP_WIDTH) + b_dw
    return jax.nn.silu(_ln(y, ln_g, ln_b)), z_ext[:, -CONV_STATE:]


def _mem_kv(mem, g, w_k, w_v):
    m = _rms(mem, g)
    return m @ w_k, m @ w_v


def _cross_attn(h, mem_k, mem_v, w_q, w_o):
    n, L, _ = h.shape
    q = (h @ w_q).reshape(n, L, MEM_HEADS, MEM_HEAD_DIM)
    k = mem_k.reshape(n, -1, MEM_HEADS, MEM_HEAD_DIM)
    v = mem_v.reshape(n, -1, MEM_HEADS, MEM_HEAD_DIM)
    s = jnp.einsum('nlhd,nmhd->nhlm', q, k, preferred_element_type=jnp.float32) * MEM_HEAD_DIM ** -0.5
    p = jax.nn.softmax(s, axis=-1).astype(v.dtype)
    return jnp.einsum('nhlm,nmhd->nlhd', p, v).reshape(n, L, MEM_WIDTH) @ w_o


def _layer(x, pos, mem_k, mem_v, pool_prefix, conv_prefix, mla_past, lp):
    n, L, _ = x.shape
    x = x + 0.5 * _swiglu(_rms(x, lp['norm_ffn_a']), lp['w_ffn_a_in'], lp['w_ffn_a_out'])
    h = _rms(x, lp['norm_mix']) @ lp['w_in']
    h_a, h_b, h_c, h_d = jnp.split(h, [A_COLS, A_COLS + B_COLS, A_COLS + B_COLS + C_COLS], axis=-1)
    q_lat, q_rope, c_kv, k_r = _mla_project(h_a, pos, lp['norm_q_lat'], lp['w_qb'], lp['norm_kv_lat'], lp['w_uk'])
    if mla_past is None:
        o_lat = _mla_prompt(q_lat, q_rope, c_kv, k_r)
    else:
        o_lat = _mla_sample(q_lat, q_rope, c_kv, k_r, mla_past[0], mla_past[1])
    y_a = jnp.einsum('nlhc,chv->nlhv', o_lat, lp['w_uv']).reshape(n, L, GROUP_WIDTH)
    y_b, v_b = _spatial_gate(h_b, lp['sg_w'], lp['sg_b'], lp['sg_ln_g'], lp['sg_ln_b'])
    y_c, pool_state = _pool_mix(h_c, pool_prefix, pos, lp['pool_w'], lp['pool_scale'])
    y_d, conv_state = _conv_module(h_d, conv_prefix, lp['conv_w'], lp['conv_b'], lp['conv_ln_g'], lp['conv_ln_b'])
    x = x + jnp.concatenate([y_a, y_b, y_c, y_d], axis=-1) @ lp['w_out']
    x = x + _cross_attn(_rms(x, lp['norm_cross']), mem_k, mem_v, lp['w_cq'], lp['w_co'])
    x = x + 0.5 * _swiglu(_rms(x, lp['norm_ffn_b']), lp['w_ffn_b_in'], lp['w_ffn_b_out'])
    return x, (c_kv, k_r, v_b, pool_state, conv_state)


def setup_inputs(seed: int = 0) -> dict:
    key = jax.random.key(seed)
    ks = iter(jax.random.split(key, 64))
    f32 = jnp.float32

    def nrm(shape, scale):
        return jax.random.normal(next(ks), shape, f32) * scale

    def gain(shape):
        return 1.0 + 0.05 * jax.random.normal(next(ks), shape, f32)

    n_pages = PAST_LEN // PAGE_SIZE
    n_pool = (DEC_BATCH * n_pages * 5) // 4
    return {
        'x_prompt': nrm((BATCH, SEQ, D_MODEL), 1.0),
        'x_sample': nrm((DEC_BATCH, DEC_SEQ, D_MODEL), 1.0),
        'mem_prompt': nrm((BATCH, MEM_LEN, D_MODEL), 1.0),
        'cache_ckv': nrm((DEPTH, n_pool, PAGE_SIZE, KV_LORA), 1.0),
        'cache_krope': nrm((DEPTH, n_pool, PAGE_SIZE, MLA_ROPE), 1.0),
        'cache_mem_k': nrm((DEPTH, DEC_BATCH, MEM_LEN, MEM_WIDTH), 1.0),
        'cache_mem_v': nrm((DEPTH, DEC_BATCH, MEM_LEN, MEM_WIDTH), 1.0),
        'state_pool': nrm((DEPTH, DEC_BATCH, POOL_STATE, GROUP_WIDTH), 1.0),
        'state_conv': nrm((DEPTH, DEC_BATCH, CONV_STATE, GROUP_WIDTH), 0.5),
        'page_table': jax.random.permutation(next(ks), n_pool)[:DEC_BATCH * n_pages]
                      .reshape(DEC_BATCH, n_pages).astype(jnp.int32),
        'w_in': nrm((DEPTH, D_MODEL, IN_COLS), D_MODEL ** -0.5),
        'w_out': nrm((DEPTH, MIX_WIDTH, D_MODEL), MIX_WIDTH ** -0.5),
        'norm_q_lat': gain((DEPTH, Q_LORA)),
        'w_qb': nrm((DEPTH, Q_LORA, MLA_HEADS, MLA_NOPE + MLA_ROPE), Q_LORA ** -0.5),
        'norm_kv_lat': gain((DEPTH, KV_LORA)),
        'w_uk': nrm((DEPTH, KV_LORA, MLA_HEADS, MLA_NOPE), KV_LORA ** -0.5),
        'w_uv': nrm((DEPTH, KV_LORA, MLA_HEADS, MLA_V), KV_LORA ** -0.5),
        'sg_w': nrm((DEPTH, SG_HEADS, CHUNK, CHUNK), CHUNK ** -0.5),
        'sg_b': 1.0 + nrm((DEPTH, SG_HEADS, CHUNK), 0.1),
        'sg_ln_g': gain((DEPTH, GROUP_WIDTH)),
        'sg_ln_b': nrm((DEPTH, GROUP_WIDTH), 0.01),
        'pool_w': nrm((DEPTH, POOL_GROUPS, POOL_GROUP, POOL_GROUP), POOL_GROUP ** -0.5),
        'pool_scale': 1.0 + nrm((DEPTH, GROUP_WIDTH), 0.1),
        'conv_w': nrm((DEPTH, CONV_WIDTH, GROUP_WIDTH), CONV_WIDTH ** -0.5),
        'conv_b': nrm((DEPTH, GROUP_WIDTH), 0.01),
        'conv_ln_g': gain((DEPTH, GROUP_WIDTH)),
        'conv_ln_b': nrm((DEPTH, GROUP_WIDTH), 0.01),
        'norm_mix': gain((DEPTH, D_MODEL)),
        'norm_ffn_a': gain((DEPTH, D_MODEL)),
        'w_ffn_a_in': nrm((DEPTH, D_MODEL, 2 * D_FF), D_MODEL ** -0.5),
        'w_ffn_a_out': nrm((DEPTH, D_FF, D_MODEL), D_FF ** -0.5),
        'norm_ffn_b': gain((DEPTH, D_MODEL)),
        'w_ffn_b_in': nrm((DEPTH, D_MODEL, 2 * D_FF), D_MODEL ** -0.5),
        'w_ffn_b_out': nrm((DEPTH, D_FF, D_MODEL), D_FF ** -0.5),
        'norm_cross': gain((DEPTH, D_MODEL)),
        'norm_mem': gain((DEPTH, D_MODEL)),
        'w_cq': nrm((DEPTH, D_MODEL, MEM_WIDTH), D_MODEL ** -0.5),
        'w_ck': nrm((DEPTH, D_MODEL, MEM_WIDTH), D_MODEL ** -0.5),
        'w_cv': nrm((DEPTH, D_MODEL, MEM_WIDTH), D_MODEL ** -0.5),
        'w_co': nrm((DEPTH, MEM_WIDTH, D_MODEL), MEM_WIDTH ** -0.5),
        'norm_final': gain((D_MODEL,)),
    }


def reference(x_prompt, x_sample, mem_prompt, cache_ckv, cache_krope, cache_mem_k, cache_mem_v,
              state_pool, state_conv, page_table, w_in, w_out, norm_q_lat, w_qb, norm_kv_lat, w_uk, w_uv,
              sg_w, sg_b, sg_ln_g, sg_ln_b, pool_w, pool_scale, conv_w, conv_b, conv_ln_g, conv_ln_b,
              norm_mix, norm_ffn_a, w_ffn_a_in, w_ffn_a_out, norm_ffn_b, w_ffn_b_in, w_ffn_b_out,
              norm_cross, norm_mem, w_cq, w_ck, w_cv, w_co, norm_final):
    n_b, seq = x_prompt.shape[:2]
    n_db, t_new = x_sample.shape[:2]
    past_len = page_table.shape[1] * cache_ckv.shape[2]
    pos_p = jnp.arange(seq)
    pos_s = past_len + jnp.arange(t_new)
    pool_zero = jnp.zeros((n_b, POOL_STATE, GROUP_WIDTH), x_prompt.dtype)
    conv_zero = jnp.zeros((n_b, CONV_STATE, GROUP_WIDTH), x_prompt.dtype)
    xp, xs = x_prompt, x_sample
    ckv_p, kr_p, mk_p, mv_p, pool_p, conv_p = [], [], [], [], [], []
    ckv_s, kr_s, sgv_s, pool_s, conv_s = [], [], [], [], []
    for l in range(DEPTH):
        lp = dict(w_in=w_in[l], w_out=w_out[l], norm_q_lat=norm_q_lat[l], w_qb=w_qb[l],
                  norm_kv_lat=norm_kv_lat[l], w_uk=w_uk[l], w_uv=w_uv[l], sg_w=sg_w[l], sg_b=sg_b[l],
                  sg_ln_g=sg_ln_g[l], sg_ln_b=sg_ln_b[l], pool_w=pool_w[l], pool_scale=pool_scale[l],
                  conv_w=conv_w[l], conv_b=conv_b[l], conv_ln_g=conv_ln_g[l], conv_ln_b=conv_ln_b[l],
                  norm_mix=norm_mix[l], norm_ffn_a=norm_ffn_a[l], w_ffn_a_in=w_ffn_a_in[l],
                  w_ffn_a_out=w_ffn_a_out[l], norm_ffn_b=norm_ffn_b[l], w_ffn_b_in=w_ffn_b_in[l],
                  w_ffn_b_out=w_ffn_b_out[l], norm_cross=norm_cross[l], w_cq=w_cq[l], w_co=w_co[l])
        mk, mv = _mem_kv(mem_prompt, norm_mem[l], w_ck[l], w_cv[l])
        xp, (ckv, kr, _, pst, cst) = _layer(xp, pos_p, mk, mv, pool_zero, conv_zero, None, lp)
        ckv_p.append(ckv); kr_p.append(kr); mk_p.append(mk); mv_p.append(mv)
        pool_p.append(pst); conv_p.append(cst)
        ckv_past = cache_ckv[l][page_table].reshape(n_db, past_len, KV_LORA)
        kr_past = cache_krope[l][page_table].reshape(n_db, past_len, MLA_ROPE)
        xs, (ckv, kr, sgv, pst, cst) = _layer(xs, pos_s, cache_mem_k[l], cache_mem_v[l], state_pool[l],
                                              state_conv[l], (ckv_past, kr_past), lp)
        ckv_s.append(ckv); kr_s.append(kr); sgv_s.append(sgv); pool_s.append(pst); conv_s.append(cst)
    y_prompt = _rms(xp, norm_final)
    y_sample = _rms(xs, norm_final)
    return (y_prompt, y_sample,
            jnp.stack(ckv_p), jnp.stack(kr_p), jnp.stack(mk_p), jnp.stack(mv_p), jnp.stack(pool_p), jnp.stack(conv_p),
            jnp.stack(ckv_s), jnp.stack(kr_s), jnp.stack(sgv_s), jnp.stack(pool_s), jnp.stack(conv_s))
```

```python
import functools

import jax
import jax.numpy as jnp
from jax import lax
from jax.experimental import pallas as pl
from jax.experimental.pallas import tpu as pltpu

F32 = jnp.float32
BF16 = jnp.bfloat16

GROUP_WIDTH = 512
MLA_HEADS = 4
MLA_NOPE = 128
MLA_ROPE = 64
KV_LORA = 256
Q_LORA = 512
MLA_SCALE = (MLA_NOPE + MLA_ROPE) ** -0.5
ROPE_THETA = 10000.0
SG_HEADS = 4
SG_HEAD_DIM = 128
CHUNK = 128
POOL_WINDOWS = (2, 4, 8, 16)
POOL_GROUP = 128
POOL_STATE = 15
CONV_WIDTH = 31
CONV_STATE = 30
MEM_HEADS = 4
MEM_HEAD_DIM = 128
MEM_WIDTH = 512
EPS = 1e-6
NEG = -1e30

LANES = 128
SUBLANES = 8
V7X_VMEM_BYTES = 64 * 2**20

ROPE_PAD = LANES
COL_B = 0
COL_D = 1024
COL_QC = 2048
COL_C = 2560
COL_CKV = 3072
COL_KR = 3328
COL_KRP = 3456
H_COLS = 3584
QK_WIDTH = KV_LORA + ROPE_PAD
HALO_POOL = 16
HALO_CONV = 32


def _params(semantics, vmem_bytes):
    return pltpu.CompilerParams(dimension_semantics=semantics,
                                vmem_limit_bytes=int(min(vmem_bytes, V7X_VMEM_BYTES - 8 * 2**20)))


def _rms(x, g):
    return x * lax.rsqrt(jnp.mean(x * x, axis=-1, keepdims=True) + EPS) * g


def _ln(x, g, b):
    mu = jnp.mean(x, axis=-1, keepdims=True)
    xc = x - mu
    return xc * lax.rsqrt(jnp.mean(xc * xc, axis=-1, keepdims=True) + EPS) * g + b


def _dot(a, b):
    return jnp.dot(a, b, preferred_element_type=F32)


def _dot_nt(a, b):
    return lax.dot_general(a, b, (((1,), (1,)), ((), ())), preferred_element_type=F32)


def _softmax(s):
    e = jnp.exp(s - jnp.max(s, axis=-1, keepdims=True))
    return e / jnp.sum(e, axis=-1, keepdims=True)


def _ffn_body(x_ref, g_ref, wg_ref, wu_ref, wo_ref, gf_ref, o_ref, xn_ref, *, final_norm):
    j = pl.program_id(1)

    @pl.when(j == 0)
    def _():
        xn_ref[...] = _rms(x_ref[...], g_ref[...]).astype(BF16)

    xn = xn_ref[...]
    hg = _dot(xn, wg_ref[...])
    hu = _dot(xn, wu_ref[...])
    act = (hg * jax.nn.sigmoid(hg) * (0.5 * hu)).astype(BF16)
    contrib = _dot(act, wo_ref[...])

    @pl.when(j == 0)
    def _():
        o_ref[...] = x_ref[...] + contrib

    @pl.when(j > 0)
    def _():
        o_ref[...] += contrib

    if final_norm:
        @pl.when(j == pl.num_programs(1) - 1)
        def _():
            o_ref[...] = _rms(o_ref[...], gf_ref[...])


def _ffn(x, gain, w_in, w_out, g_final, layer, *, final_norm, tm, tf):
    m, d = x.shape
    f = w_out.shape[1]
    nj = f // tf
    vmem = 2 * 2 * tm * d * 4 + tm * d * 2 + 2 * 3 * d * tf * 2 + tm * d * 4 + 3 * tm * tf * 4
    return pl.pallas_call(
        functools.partial(_ffn_body, final_norm=final_norm),
        out_shape=jax.ShapeDtypeStruct((m, d), F32),
        grid=(m // tm, nj),
        in_specs=[
            pl.BlockSpec((tm, d), lambda i, j: (i, 0)),
            pl.BlockSpec((None, 1, d), lambda i, j: (layer, 0, 0)),
            pl.BlockSpec((None, d, tf), lambda i, j: (layer, 0, j)),
            pl.BlockSpec((None, d, tf), lambda i, j: (layer, 0, j + nj)),
            pl.BlockSpec((None, tf, d), lambda i, j: (layer, j, 0)),
            pl.BlockSpec((1, d), lambda i, j: (0, 0)),
        ],
        out_specs=pl.BlockSpec((tm, d), lambda i, j: (i, 0)),
        scratch_shapes=[pltpu.VMEM((tm, d), BF16)],
        compiler_params=_params(("parallel", "arbitrary"), vmem + 4 * 2**20),
        name="ffn",
    )(x, gain, w_in, w_in, w_out, g_final)


def _proj_body(x_ref, g_ref, w_ref, o_ref, xn_ref):
    @pl.when(pl.program_id(1) == 0)
    def _():
        xn_ref[...] = _rms(x_ref[...], g_ref[...]).astype(BF16)

    o_ref[...] = _dot(xn_ref[...], w_ref[...])


def _rms_proj(x, gain, w, layer, *, tm, tn):
    m, d = x.shape
    n = w.shape[2]
    vmem = 2 * tm * d * 4 + tm * d * 2 + 2 * d * tn * 2 + 3 * tm * tn * 4
    return pl.pallas_call(
        _proj_body,
        out_shape=jax.ShapeDtypeStruct((m, n), F32),
        grid=(m // tm, n // tn),
        in_specs=[
            pl.BlockSpec((tm, d), lambda i, j: (i, 0)),
            pl.BlockSpec((None, 1, d), lambda i, j: (layer, 0, 0)),
            pl.BlockSpec((None, d, tn), lambda i, j: (layer, 0, j)),
        ],
        out_specs=pl.BlockSpec((tm, tn), lambda i, j: (i, j)),
        scratch_shapes=[pltpu.VMEM((tm, d), BF16)],
        compiler_params=_params(("parallel", "arbitrary"), vmem + 4 * 2**20),
        name="mix_in_proj",
    )(x, gain, w)


def _memkv_body(x_ref, g_ref, wk_ref, wv_ref, k_ref, v_ref):
    xn = _rms(x_ref[...], g_ref[...]).astype(BF16)
    k_ref[...] = _dot(xn, wk_ref[...])
    v_ref[...] = _dot(xn, wv_ref[...])


def _mem_kv(mem, gain, w_ck, w_cv, *, tm):
    m, d = mem.shape
    depth, _, n = w_ck.shape
    vmem = 2 * tm * d * 4 + tm * d * 2 + 2 * 2 * d * n * 2 + 2 * 2 * tm * n * 4
    out = jax.ShapeDtypeStruct((depth, m, n), F32)
    return pl.pallas_call(
        _memkv_body,
        out_shape=(out, out),
        grid=(depth, m // tm),
        in_specs=[
            pl.BlockSpec((tm, d), lambda l, i: (i, 0)),
            pl.BlockSpec((None, 1, d), lambda l, i: (l, 0, 0)),
            pl.BlockSpec((None, d, n), lambda l, i: (l, 0, 0)),
            pl.BlockSpec((None, d, n), lambda l, i: (l, 0, 0)),
        ],
        out_specs=(pl.BlockSpec((None, tm, n), lambda l, i: (l, i, 0)),
                   pl.BlockSpec((None, tm, n), lambda l, i: (l, i, 0))),
        compiler_params=_params(("parallel", "parallel"), vmem + 4 * 2**20),
        name="mem_kv",
    )(mem, gain, w_ck, w_cv)


def _mla_proj_body(qc_ref, ckv_ref, kr_ref, krp_ref, cos_ref, sin_ref, nq_ref, nkv_ref, wn_ref, wr_ref, wrp_ref,
                   wuk_ref, q_ref, k_ref, ckv_out_ref, kr_out_ref):
    cos = cos_ref[...]
    sin = sin_ref[...]
    qn = _rms(qc_ref[...], nq_ref[...]).astype(BF16)
    q_nope = _dot(qn, wn_ref[...]).astype(BF16)
    q_r = _dot(qn, wr_ref[...])
    q_rp = _dot(qn, wrp_ref[...])
    for h in range(MLA_HEADS):
        q_lat = _dot(q_nope[:, h * MLA_NOPE:(h + 1) * MLA_NOPE], wuk_ref[h])
        q_ref[:, h * QK_WIDTH:h * QK_WIDTH + KV_LORA] = (q_lat * MLA_SCALE).astype(BF16)
        sl = slice(h * ROPE_PAD, (h + 1) * ROPE_PAD)
        rot = q_r[:, sl] * cos + q_rp[:, sl] * sin
        q_ref[:, h * QK_WIDTH + KV_LORA:(h + 1) * QK_WIDTH] = (rot * MLA_SCALE).astype(BF16)
    ckv_n = _rms(ckv_ref[...], nkv_ref[...])
    ckv_out_ref[...] = ckv_n
    k_ref[:, :KV_LORA] = ckv_n.astype(BF16)
    k_rot = kr_ref[...] * cos + krp_ref[...] * sin
    kr_out_ref[...] = k_rot[:, :MLA_ROPE]
    k_ref[:, KV_LORA:] = k_rot.astype(BF16)


def _mla_proj(h, cos, sin, norm_q, norm_kv, w_nope, w_rope, w_ropep, w_ukt, layer, *, tm):
    m = h.shape[0]
    qw = MLA_HEADS * QK_WIDTH
    const = lambda *shape: pl.BlockSpec((None,) + shape, lambda i: (layer,) + (0,) * len(shape))
    vmem = 2 * tm * (Q_LORA + KV_LORA + 4 * ROPE_PAD) * 4 + 2 * tm * (qw + QK_WIDTH) * 2 \
        + 2 * tm * (KV_LORA + LANES) * 4 + 2 * (3 * Q_LORA * 512 + 4 * 128 * 256) * 2 + 8 * tm * 512 * 4
    return pl.pallas_call(
        _mla_proj_body,
        out_shape=(jax.ShapeDtypeStruct((m, qw), BF16), jax.ShapeDtypeStruct((m, QK_WIDTH), BF16),
                   jax.ShapeDtypeStruct((m, KV_LORA), F32), jax.ShapeDtypeStruct((m, MLA_ROPE), F32)),
        grid=(m // tm,),
        in_specs=[
            pl.BlockSpec((tm, Q_LORA), lambda i: (i, COL_QC // Q_LORA)),
            pl.BlockSpec((tm, KV_LORA), lambda i: (i, COL_CKV // KV_LORA)),
            pl.BlockSpec((tm, ROPE_PAD), lambda i: (i, COL_KR // ROPE_PAD)),
            pl.BlockSpec((tm, ROPE_PAD), lambda i: (i, COL_KRP // ROPE_PAD)),
            pl.BlockSpec((tm, ROPE_PAD), lambda i: (i, 0)),
            pl.BlockSpec((tm, ROPE_PAD), lambda i: (i, 0)),
            const(1, Q_LORA), const(1, KV_LORA),
            const(Q_LORA, MLA_HEADS * MLA_NOPE), const(Q_LORA, MLA_HEADS * ROPE_PAD),
            const(Q_LORA, MLA_HEADS * ROPE_PAD), const(MLA_HEADS, MLA_NOPE, KV_LORA),
        ],
        out_specs=(pl.BlockSpec((tm, qw), lambda i: (i, 0)), pl.BlockSpec((tm, QK_WIDTH), lambda i: (i, 0)),
                   pl.BlockSpec((tm, KV_LORA), lambda i: (i, 0)), pl.BlockSpec((tm, MLA_ROPE), lambda i: (i, 0))),
        compiler_params=_params(("parallel",), vmem + 4 * 2**20),
        name="mla_proj",
    )(h, h, h, h, cos, sin, norm_q, norm_kv, w_nope, w_rope, w_ropep, w_ukt)


def _mla_prompt_body(q_ref, k_ref, o_ref, m_ref, l_ref, acc_ref, *, tq):
    qi = pl.program_id(1)
    ki = pl.program_id(2)

    @pl.when(ki == 0)
    def _():
        m_ref[...] = jnp.full(m_ref.shape, NEG, F32)
        l_ref[...] = jnp.zeros(l_ref.shape, F32)
        acc_ref[...] = jnp.zeros(acc_ref.shape, F32)

    @pl.when(ki <= qi)
    def _():
        k = k_ref[...]
        v = k[:, :KV_LORA]
        q_pos = qi * tq + lax.broadcasted_iota(jnp.int32, (tq, tq), 0)
        k_pos = ki * tq + lax.broadcasted_iota(jnp.int32, (tq, tq), 1)
        allowed = k_pos <= q_pos
        for h in range(MLA_HEADS):
            s = _dot_nt(q_ref[:, h * QK_WIDTH:(h + 1) * QK_WIDTH], k)
            s = jnp.where(allowed, s, NEG)
            m_prev = m_ref[h][:, :1]
            l_prev = l_ref[h][:, :1]
            m_new = jnp.maximum(m_prev, jnp.max(s, axis=-1, keepdims=True))
            alpha = jnp.exp(m_prev - m_new)
            p = jnp.exp(s - m_new)
            l_new = alpha * l_prev + jnp.sum(p, axis=-1, keepdims=True)
            acc_ref[h] = alpha * acc_ref[h] + _dot(p.astype(BF16), v)
            m_ref[h] = jnp.broadcast_to(m_new, (tq, LANES))
            l_ref[h] = jnp.broadcast_to(l_new, (tq, LANES))

    @pl.when(ki == qi)
    def _():
        for h in range(MLA_HEADS):
            o_ref[:, h * KV_LORA:(h + 1) * KV_LORA] = (acc_ref[h] / l_ref[h][:, :1]).astype(BF16)


def _mla_prompt(q_full, k_full, n_batch, *, tq):
    m = q_full.shape[0]
    nq = m // n_batch // tq
    qw = MLA_HEADS * QK_WIDTH
    ow = MLA_HEADS * KV_LORA
    vmem = 2 * tq * qw * 2 + 2 * tq * QK_WIDTH * 2 + 2 * tq * ow * 2 \
        + MLA_HEADS * tq * (2 * LANES + KV_LORA) * 4 + 6 * tq * tq * 4
    return pl.pallas_call(
        functools.partial(_mla_prompt_body, tq=tq),
        out_shape=jax.ShapeDtypeStruct((m, ow), BF16),
        grid=(n_batch, nq, nq),
        in_specs=[
            pl.BlockSpec((tq, qw), lambda b, qi, ki: (b * nq + qi, 0)),
            pl.BlockSpec((tq, QK_WIDTH), lambda b, qi, ki: (b * nq + jnp.minimum(ki, qi), 0)),
        ],
        out_specs=pl.BlockSpec((tq, ow), lambda b, qi, ki: (b * nq + qi, 0)),
        scratch_shapes=[pltpu.VMEM((MLA_HEADS, tq, LANES), F32), pltpu.VMEM((MLA_HEADS, tq, LANES), F32),
                        pltpu.VMEM((MLA_HEADS, tq, KV_LORA), F32)],
        compiler_params=_params(("parallel", "parallel", "arbitrary"), vmem + 4 * 2**20),
        name="mla_prompt_attn",
    )(q_full, k_full)


def _page_copy(src_hbm, dst_buf, sem, layer, page, slot, j, page_size):
    row0 = pl.multiple_of(j * page_size, page_size)
    return pltpu.make_async_copy(src_hbm.at[layer, page], dst_buf.at[slot, pl.ds(row0, page_size)], sem.at[slot])


def _mla_sample_body(pt_ref, q_ref, kn_ref, ckv_hbm, kr_hbm, o_ref, ckv_buf, kr_buf, kbf_ref, s_ref, sem_c, sem_r, *,
                     layer, n_pages, page_size, chunk, t_new):
    n = pl.program_id(0)
    n_samples = pl.num_programs(0)
    past = n_pages * page_size

    def start_fetch(sample, slot):
        def body(j, c):
            page = pt_ref[sample, j]
            _page_copy(ckv_hbm, ckv_buf, sem_c, layer, page, slot, j, page_size).start()
            _page_copy(kr_hbm, kr_buf, sem_r, layer, page, slot, j, page_size).start()
            return c
        lax.fori_loop(0, n_pages, body, 0)

    def wait_fetch(slot):
        def body(j, c):
            _page_copy(ckv_hbm, ckv_buf, sem_c, layer, 0, slot, j, page_size).wait()
            _page_copy(kr_hbm, kr_buf, sem_r, layer, 0, slot, j, page_size).wait()
            return c
        lax.fori_loop(0, n_pages, body, 0)

    slot = lax.rem(n, 2)

    @pl.when(n == 0)
    def _():
        start_fetch(0, 0)

    @pl.when(n + 1 < n_samples)
    def _():
        start_fetch(n + 1, 1 - slot)

    wait_fetch(slot)

    q = q_ref[...]
    q_lat = q[:, :KV_LORA]
    q_rope = q[:, KV_LORA:KV_LORA + MLA_ROPE]
    rows = q.shape[0]
    for c in range(past // chunk):
        sl = pl.ds(c * chunk, chunk)
        kc = ckv_buf[slot, sl, :].astype(BF16)
        kbf_ref[sl, :] = kc
        rc = kr_buf[slot, sl, :].astype(BF16)
        s_ref[:, c * chunk:(c + 1) * chunk] = _dot_nt(q_lat, kc) + _dot_nt(q_rope, rc)

    kn = kn_ref[...]
    s_new = _dot_nt(q, kn)
    q_tok = lax.broadcasted_iota(jnp.int32, (rows, t_new), 0) // MLA_HEADS
    k_tok = lax.broadcasted_iota(jnp.int32, (rows, t_new), 1)
    s_new = jnp.where(k_tok <= q_tok, s_new, NEG)

    s_past = s_ref[...]
    m = jnp.maximum(jnp.max(s_past, axis=-1, keepdims=True), jnp.max(s_new, axis=-1, keepdims=True))
    p_past = jnp.exp(s_past - m)
    p_new = jnp.exp(s_new - m)
    denom = jnp.sum(p_past, axis=-1, keepdims=True) + jnp.sum(p_new, axis=-1, keepdims=True)
    p_past = (p_past / denom).astype(BF16)
    p_new = (p_new / denom).astype(BF16)
    o = _dot(p_new, kn[:, :KV_LORA])
    for c in range(past // chunk):
        o = o + _dot(p_past[:, c * chunk:(c + 1) * chunk], kbf_ref[c * chunk:(c + 1) * chunk, :])
    o_ref[...] = o.astype(BF16)


def _mla_sample(page_table, q3, kn3, cache_ckv, cache_krope, layer, *, chunk):
    n_samples, rows, _ = q3.shape
    t_new = kn3.shape[1]
    n_pages = page_table.shape[1]
    page_size = cache_ckv.shape[2]
    past = n_pages * page_size
    vmem = 2 * past * (KV_LORA + LANES) * 4 + past * KV_LORA * 2 + rows * past * 4 * 4 + 4 * chunk * 384 * 4
    grid_spec = pltpu.PrefetchScalarGridSpec(
        num_scalar_prefetch=1,
        grid=(n_samples,),
        in_specs=[
            pl.BlockSpec((None, rows, QK_WIDTH), lambda n, pt: (n, 0, 0)),
            pl.BlockSpec((None, t_new, QK_WIDTH), lambda n, pt: (n, 0, 0)),
            pl.BlockSpec(memory_space=pl.ANY),
            pl.BlockSpec(memory_space=pl.ANY),
        ],
        out_specs=pl.BlockSpec((None, rows, KV_LORA), lambda n, pt: (n, 0, 0)),
        scratch_shapes=[
            pltpu.VMEM((2, past, KV_LORA), F32),
            pltpu.VMEM((2, past, MLA_ROPE), F32),
            pltpu.VMEM((past, KV_LORA), BF16),
            pltpu.VMEM((rows, past), F32),
            pltpu.SemaphoreType.DMA((2,)),
            pltpu.SemaphoreType.DMA((2,)),
        ],
    )
    return pl.pallas_call(
        functools.partial(_mla_sample_body, layer=layer, n_pages=n_pages, page_size=page_size, chunk=chunk,
                          t_new=t_new),
        out_shape=jax.ShapeDtypeStruct((n_samples, rows, KV_LORA), BF16),
        grid_spec=grid_spec,
        compiler_params=_params(("arbitrary",), vmem + 4 * 2**20),
        name="mla_sample_attn",
    )(page_table, q3, kn3, cache_ckv, cache_krope)


def _sg_body(h_ref, w_ref, b_ref, g_ref, beta_ref, y_ref, *v_refs, period, tm):
    row = lax.broadcasted_iota(jnp.int32, (CHUNK, CHUNK), 0)
    col = lax.broadcasted_iota(jnp.int32, (CHUNK, CHUNK), 1)
    shift = period.bit_length() - 1
    keep = (col <= row) & ((row >> shift) == (col >> shift))
    w = [jnp.where(keep, w_ref[h], 0.0).astype(BF16) for h in range(SG_HEADS)]
    bias = b_ref[...]
    for c in range(tm // CHUNK):
        rows = pl.ds(c * CHUNK, CHUNK)
        a = jax.nn.gelu(h_ref[rows, :])
        u = a[:, :GROUP_WIDTH]
        v = _ln(a[:, GROUP_WIDTH:], g_ref[...], beta_ref[...])
        if v_refs:
            v_refs[0][rows, :] = v
        vb = v.astype(BF16)
        for h in range(SG_HEADS):
            sl = slice(h * SG_HEAD_DIM, (h + 1) * SG_HEAD_DIM)
            g = _dot(w[h], vb[:, sl]) + bias[:, sl]
            y_ref[rows, sl] = (u[:, sl] * g).astype(BF16)


def _spatial_gate(h, w, b_rows, ln_g, ln_b, layer, *, period, with_v, tm):
    m = h.shape[0]
    const = lambda *shape: pl.BlockSpec((None,) + shape, lambda i: (layer,) + (0,) * len(shape))
    out_shape = [jax.ShapeDtypeStruct((m, GROUP_WIDTH), BF16)]
    out_specs = [pl.BlockSpec((tm, GROUP_WIDTH), lambda i: (i, 0))]
    if with_v:
        out_shape.append(jax.ShapeDtypeStruct((m, GROUP_WIDTH), F32))
        out_specs.append(pl.BlockSpec((tm, GROUP_WIDTH), lambda i: (i, 0)))
    vmem = 2 * tm * 2 * GROUP_WIDTH * 4 + 2 * tm * GROUP_WIDTH * 6 + 16 * CHUNK * 2 * GROUP_WIDTH * 4
    return pl.pallas_call(
        functools.partial(_sg_body, period=period, tm=tm),
        out_shape=tuple(out_shape),
        grid=(m // tm,),
        in_specs=[
            pl.BlockSpec((tm, 2 * GROUP_WIDTH), lambda i: (i, COL_B // (2 * GROUP_WIDTH))),
            const(SG_HEADS, CHUNK, CHUNK), const(CHUNK, GROUP_WIDTH), const(1, GROUP_WIDTH), const(1, GROUP_WIDTH),
        ],
        out_specs=tuple(out_specs),
        compiler_params=_params(("parallel",), vmem + 4 * 2**20),
        name="spatial_gate",
    )(h, w, b_rows, ln_g, ln_b)


def _conv_tail(y, ln_g, ln_b):
    return jax.nn.silu(_ln(y, ln_g, ln_b))


def _pc_prompt_body(hc_ref, hd_ref, pw_ref, ps_ref, cw_ref, cb_ref, g_ref, beta_ref,
                    yc_ref, yd_ref, pool_out_ref, conv_out_ref, cext, zext, ybuf, *, tl, rb):
    li = pl.program_id(1)

    @pl.when(li == 0)
    def _():
        cext[0:HALO_POOL, :] = jnp.zeros((HALO_POOL, GROUP_WIDTH), F32)
        zext[0:HALO_CONV, :] = jnp.zeros((HALO_CONV, GROUP_WIDTH), F32)

    hc = hc_ref[...]
    cext[HALO_POOL:HALO_POOL + tl, :] = hc
    pos = li * tl + lax.broadcasted_iota(jnp.int32, (tl, 1), 0)
    for gi, win in enumerate(POOL_WINDOWS):
        sl = slice(gi * POOL_GROUP, (gi + 1) * POOL_GROUP)
        acc = hc[:, sl]
        for k in range(1, win):
            acc = acc + cext[pl.ds(HALO_POOL - k, tl), sl]
        cnt = jnp.minimum(pos + 1, win).astype(F32)
        pooled = acc / cnt - hc[:, sl]
        y = _dot(pooled.astype(BF16), pw_ref[gi]) * ps_ref[:, sl]
        yc_ref[:, sl] = y.astype(BF16)

    hd = hd_ref[...]
    zext[HALO_CONV:HALO_CONV + tl, :] = hd[:, :GROUP_WIDTH] * jax.nn.sigmoid(hd[:, GROUP_WIDTH:])
    first = HALO_CONV - CONV_STATE
    for r0 in range(0, tl, rb):
        for cg in range(GROUP_WIDTH // LANES):
            sl = slice(cg * LANES, (cg + 1) * LANES)
            acc = jnp.broadcast_to(cb_ref[:, sl], (rb, LANES))
            for k in range(CONV_WIDTH):
                acc = acc + cw_ref[k:k + 1, sl] * zext[pl.ds(first + r0 + k, rb), sl]
            ybuf[r0:r0 + rb, sl] = acc
    yd_ref[...] = _conv_tail(ybuf[...], g_ref[...], beta_ref[...]).astype(BF16)

    @pl.when(li == pl.num_programs(1) - 1)
    def _():
        pool_out_ref[...] = cext[pl.ds(HALO_POOL + tl - POOL_STATE, POOL_STATE), :]
        conv_out_ref[...] = zext[pl.ds(HALO_CONV + tl - CONV_STATE, CONV_STATE), :]

    cext[0:HALO_POOL, :] = cext[tl:tl + HALO_POOL, :]
    zext[0:HALO_CONV, :] = zext[tl:tl + HALO_CONV, :]


def _pool_conv_prompt(h, n_batch, pool_w, pool_scale, conv_w, conv_b, ln_g, ln_b, layer, *, tl, rb):
    m = h.shape[0]
    nl = m // n_batch // tl
    const = lambda *shape: pl.BlockSpec((None,) + shape, lambda b, l: (layer,) + (0,) * len(shape))
    y = jax.ShapeDtypeStruct((m, GROUP_WIDTH), BF16)
    vmem = 2 * tl * 3 * GROUP_WIDTH * 4 + 2 * 2 * tl * GROUP_WIDTH * 2 + 3 * (tl + HALO_CONV) * GROUP_WIDTH * 4 \
        + 8 * tl * GROUP_WIDTH * 4
    return pl.pallas_call(
        functools.partial(_pc_prompt_body, tl=tl, rb=rb),
        out_shape=(y, y, jax.ShapeDtypeStruct((n_batch, POOL_STATE, GROUP_WIDTH), F32),
                   jax.ShapeDtypeStruct((n_batch, CONV_STATE, GROUP_WIDTH), F32)),
        grid=(n_batch, nl),
        in_specs=[
            pl.BlockSpec((tl, GROUP_WIDTH), lambda b, l: (b * nl + l, COL_C // GROUP_WIDTH)),
            pl.BlockSpec((tl, 2 * GROUP_WIDTH), lambda b, l: (b * nl + l, COL_D // (2 * GROUP_WIDTH))),
            const(len(POOL_WINDOWS), POOL_GROUP, POOL_GROUP), const(1, GROUP_WIDTH),
            const(CONV_WIDTH, GROUP_WIDTH), const(1, GROUP_WIDTH), const(1, GROUP_WIDTH), const(1, GROUP_WIDTH),
        ],
        out_specs=(pl.BlockSpec((tl, GROUP_WIDTH), lambda b, l: (b * nl + l, 0)),
                   pl.BlockSpec((tl, GROUP_WIDTH), lambda b, l: (b * nl + l, 0)),
                   pl.BlockSpec((None, POOL_STATE, GROUP_WIDTH), lambda b, l: (b, 0, 0)),
                   pl.BlockSpec((None, CONV_STATE, GROUP_WIDTH), lambda b, l: (b, 0, 0))),
        scratch_shapes=[pltpu.VMEM((HALO_POOL + tl, GROUP_WIDTH), F32), pltpu.VMEM((HALO_CONV + tl, GROUP_WIDTH), F32),
                        pltpu.VMEM((tl, GROUP_WIDTH), F32)],
        compiler_params=_params(("parallel", "arbitrary"), vmem + 4 * 2**20),
        name="pool_conv_prompt",
    )(h, h, pool_w, pool_scale, conv_w, conv_b, ln_g, ln_b)


def _pc_sample_body(hc_ref, hd_ref, sp_ref, sc_ref, pw_ref, ps_ref, cw_ref, cb_ref, g_ref, beta_ref,
                    yc_ref, yd_ref, pool_out_ref, conv_out_ref, cext, zext, ybuf, *, bs, t_new, past_len):
    rows = bs * t_new
    hc = hc_ref[...]
    cext[:, 0:POOL_STATE, :] = sp_ref[...]
    cext[:, POOL_STATE:POOL_STATE + t_new, :] = hc.reshape(bs, t_new, GROUP_WIDTH)
    pos = past_len + lax.broadcasted_iota(jnp.int32, (1, t_new, 1), 1)
    for gi, win in enumerate(POOL_WINDOWS):
        sl = slice(gi * POOL_GROUP, (gi + 1) * POOL_GROUP)
        acc = cext[:, pl.ds(POOL_STATE, t_new), sl]
        for k in range(1, win):
            acc = acc + cext[:, pl.ds(POOL_STATE - k, t_new), sl]
        cnt = jnp.minimum(pos + 1, win).astype(F32)
        pooled = (acc / cnt).reshape(rows, POOL_GROUP) - hc[:, sl]
        y = _dot(pooled.astype(BF16), pw_ref[gi]) * ps_ref[:, sl]
        yc_ref[:, sl] = y.astype(BF16)
    pool_out_ref[...] = cext[:, pl.ds(t_new, POOL_STATE), :]

    hd = hd_ref[...]
    z = hd[:, :GROUP_WIDTH] * jax.nn.sigmoid(hd[:, GROUP_WIDTH:])
    zext[:, 0:CONV_STATE, :] = sc_ref[...]
    zext[:, CONV_STATE:CONV_STATE + t_new, :] = z.reshape(bs, t_new, GROUP_WIDTH)
    for cg in range(GROUP_WIDTH // LANES):
        sl = slice(cg * LANES, (cg + 1) * LANES)
        acc = jnp.broadcast_to(cb_ref[:, sl].reshape(1, 1, LANES), (bs, t_new, LANES))
        for k in range(CONV_WIDTH):
            acc = acc + cw_ref[k:k + 1, sl].reshape(1, 1, LANES) * zext[:, pl.ds(k, t_new), sl]
        ybuf[:, sl] = acc.reshape(rows, LANES)
    yd_ref[...] = _conv_tail(ybuf[...], g_ref[...], beta_ref[...]).astype(BF16)
    conv_out_ref[...] = zext[:, pl.ds(t_new, CONV_STATE), :]


def _pool_conv_sample(h, state_pool, state_conv, pool_w, pool_scale, conv_w, conv_b, ln_g, ln_b, layer, *,
                      bs, t_new, past_len):
    m = h.shape[0]
    n_samples = m // t_new
    rows = bs * t_new
    const = lambda *shape: pl.BlockSpec((None,) + shape, lambda i: (layer,) + (0,) * len(shape))
    y = jax.ShapeDtypeStruct((m, GROUP_WIDTH), BF16)
    pool_rows = POOL_STATE + t_new + 1
    conv_rows = CONV_STATE + t_new + 2
    vmem = 2 * rows * 3 * GROUP_WIDTH * 4 + 4 * bs * (16 + 32) * GROUP_WIDTH * 4 \
        + bs * (pool_rows + conv_rows) * GROUP_WIDTH * 4 + 8 * rows * GROUP_WIDTH * 4
    return pl.pallas_call(
        functools.partial(_pc_sample_body, bs=bs, t_new=t_new, past_len=past_len),
        out_shape=(y, y, jax.ShapeDtypeStruct((n_samples, POOL_STATE, GROUP_WIDTH), F32),
                   jax.ShapeDtypeStruct((n_samples, CONV_STATE, GROUP_WIDTH), F32)),
        grid=(n_samples // bs,),
        in_specs=[
            pl.BlockSpec((rows, GROUP_WIDTH), lambda i: (i, COL_C // GROUP_WIDTH)),
            pl.BlockSpec((rows, 2 * GROUP_WIDTH), lambda i: (i, COL_D // (2 * GROUP_WIDTH))),
            pl.BlockSpec((None, bs, POOL_STATE, GROUP_WIDTH), lambda i: (layer, i, 0, 0)),
            pl.BlockSpec((None, bs, CONV_STATE, GROUP_WIDTH), lambda i: (layer, i, 0, 0)),
            const(len(POOL_WINDOWS), POOL_GROUP, POOL_GROUP), const(1, GROUP_WIDTH),
            const(CONV_WIDTH, GROUP_WIDTH), const(1, GROUP_WIDTH), const(1, GROUP_WIDTH), const(1, GROUP_WIDTH),
        ],
        out_specs=(pl.BlockSpec((rows, GROUP_WIDTH), lambda i: (i, 0)),
                   pl.BlockSpec((rows, GROUP_WIDTH), lambda i: (i, 0)),
                   pl.BlockSpec((bs, POOL_STATE, GROUP_WIDTH), lambda i: (i, 0, 0)),
                   pl.BlockSpec((bs, CONV_STATE, GROUP_WIDTH), lambda i: (i, 0, 0))),
        scratch_shapes=[pltpu.VMEM((bs, pool_rows, GROUP_WIDTH), F32), pltpu.VMEM((bs, conv_rows, GROUP_WIDTH), F32),
                        pltpu.VMEM((rows, GROUP_WIDTH), F32)],
        compiler_params=_params(("parallel",), vmem + 4 * 2**20),
        name="pool_conv_sample",
    )(h, h, state_pool, state_conv, pool_w, pool_scale, conv_w, conv_b, ln_g, ln_b)


def _mix_out_body(x_ref, ol_ref, yb_ref, yc_ref, yd_ref, wuv_ref, wo_ref, o_ref):
    ol = ol_ref[...]
    parts = [_dot(ol[:, h * KV_LORA:(h + 1) * KV_LORA], wuv_ref[h]).astype(BF16) for h in range(MLA_HEADS)]
    y = jnp.concatenate(parts + [yb_ref[...], yc_ref[...], yd_ref[...]], axis=1)
    o_ref[...] = x_ref[...] + _dot(y, wo_ref[...])


def _mix_out(x, o_lat, y_b, y_c, y_d, w_uv, w_out, layer, *, tm):
    m, d = x.shape
    mw = w_out.shape[1]
    row = lambda w: pl.BlockSpec((tm, w), lambda i: (i, 0))
    vmem = 2 * 2 * tm * d * 4 + 2 * tm * (MLA_HEADS * KV_LORA + 3 * GROUP_WIDTH) * 2 + 2 * mw * d * 2 \
        + tm * mw * 2 + tm * d * 4
    return pl.pallas_call(
        _mix_out_body,
        out_shape=jax.ShapeDtypeStruct((m, d), F32),
        grid=(m // tm,),
        in_specs=[
            row(d), row(MLA_HEADS * KV_LORA), row(GROUP_WIDTH), row(GROUP_WIDTH), row(GROUP_WIDTH),
            pl.BlockSpec((None, MLA_HEADS, KV_LORA, 128), lambda i: (layer, 0, 0, 0)),
            pl.BlockSpec((None, mw, d), lambda i: (layer, 0, 0)),
        ],
        out_specs=row(d),
        compiler_params=_params(("parallel",), vmem + 4 * 2**20),
        name="mix_out_proj",
    )(x, o_lat, y_b, y_c, y_d, w_uv, w_out)


def _cross_prompt_body(x_ref, g_ref, wq_ref, wo_ref, mk_ref, mv_ref, o_ref):
    x = x_ref[...]
    xn = _rms(x, g_ref[...]).astype(BF16)
    q = (_dot(xn, wq_ref[...]) * MEM_HEAD_DIM ** -0.5).astype(BF16)
    k = mk_ref[...].astype(BF16)
    v = mv_ref[...].astype(BF16)
    heads = []
    for h in range(MEM_HEADS):
        sl = slice(h * MEM_HEAD_DIM, (h + 1) * MEM_HEAD_DIM)
        p = _softmax(_dot_nt(q[:, sl], k[:, sl]))
        heads.append(_dot(p.astype(BF16), v[:, sl]).astype(BF16))
    o_ref[...] = x + _dot(jnp.concatenate(heads, axis=1), wo_ref[...])


def _cross_prompt(x, gain, w_cq, w_co, mem_k, mem_v, layer, *, tq):
    m, d = x.shape
    n_batch, mem_len = mem_k.shape[1:3]
    nq = m // n_batch // tq
    const = lambda *shape: pl.BlockSpec((None,) + shape, lambda b, i: (layer,) + (0,) * len(shape))
    vmem = 2 * 2 * tq * d * 4 + 2 * 2 * d * MEM_WIDTH * 2 + 2 * 2 * mem_len * MEM_WIDTH * 4 \
        + tq * d * 6 + 8 * tq * MEM_WIDTH * 4
    return pl.pallas_call(
        _cross_prompt_body,
        out_shape=jax.ShapeDtypeStruct((m, d), F32),
        grid=(n_batch, nq),
        in_specs=[
            pl.BlockSpec((tq, d), lambda b, i: (b * nq + i, 0)),
            const(1, d), const(d, MEM_WIDTH), const(MEM_WIDTH, d),
            pl.BlockSpec((None, None, mem_len, MEM_WIDTH), lambda b, i: (layer, b, 0, 0)),
            pl.BlockSpec((None, None, mem_len, MEM_WIDTH), lambda b, i: (layer, b, 0, 0)),
        ],
        out_specs=pl.BlockSpec((tq, d), lambda b, i: (b * nq + i, 0)),
        compiler_params=_params(("parallel", "parallel"), vmem + 4 * 2**20),
        name="cross_attn_prompt",
    )(x, gain, w_cq, w_co, mem_k, mem_v)


def _cross_sample_body(x_ref, g_ref, wq_ref, wo_ref, mk_ref, mv_ref, o_ref, attn_ref, *, bs, t_new):
    x = x_ref[...]
    xn = _rms(x, g_ref[...]).astype(BF16)
    q = _dot(xn, wq_ref[...]) * MEM_HEAD_DIM ** -0.5
    rows = MEM_HEADS * t_new
    lane_head = lax.broadcasted_iota(jnp.int32, (rows, MEM_WIDTH), 1) // MEM_HEAD_DIM
    row_head = lax.broadcasted_iota(jnp.int32, (rows, MEM_WIDTH), 0) // t_new
    own = lane_head == row_head
    for s in range(bs):
        qs = q[s * t_new:(s + 1) * t_new, :]
        q_bd = jnp.where(own, jnp.concatenate([qs] * MEM_HEADS, axis=0), 0.0).astype(BF16)
        p = _softmax(_dot_nt(q_bd, mk_ref[s].astype(BF16)))
        r = jnp.where(own, _dot(p.astype(BF16), mv_ref[s].astype(BF16)), 0.0)
        o = r[0:t_new]
        for h in range(1, MEM_HEADS):
            o = o + r[h * t_new:(h + 1) * t_new]
        attn_ref[s * t_new:(s + 1) * t_new, :] = o
    o_ref[...] = x + _dot(attn_ref[...].astype(BF16), wo_ref[...])


def _cross_sample(x, gain, w_cq, w_co, mem_k, mem_v, layer, *, bs, t_new):
    m, d = x.shape
    mem_len = mem_k.shape[2]
    rows = bs * t_new
    const = lambda *shape: pl.BlockSpec((None,) + shape, lambda i: (layer,) + (0,) * len(shape))
    vmem = 2 * 2 * rows * d * 4 + 2 * 2 * d * MEM_WIDTH * 2 + 2 * 2 * bs * mem_len * MEM_WIDTH * 4 \
        + rows * d * 6 + 16 * mem_len * MEM_WIDTH * 4
    return pl.pallas_call(
        functools.partial(_cross_sample_body, bs=bs, t_new=t_new),
        out_shape=jax.ShapeDtypeStruct((m, d), F32),
        grid=(m // rows,),
        in_specs=[
            pl.BlockSpec((rows, d), lambda i: (i, 0)),
            const(1, d), const(d, MEM_WIDTH), const(MEM_WIDTH, d),
            pl.BlockSpec((None, bs, mem_len, MEM_WIDTH), lambda i: (layer, i, 0, 0)),
            pl.BlockSpec((None, bs, mem_len, MEM_WIDTH), lambda i: (layer, i, 0, 0)),
        ],
        out_specs=pl.BlockSpec((rows, d), lambda i: (i, 0)),
        scratch_shapes=[pltpu.VMEM((rows, MEM_WIDTH), F32)],
        compiler_params=_params(("parallel",), vmem + 4 * 2**20),
        name="cross_attn_sample",
    )(x, gain, w_cq, w_co, mem_k, mem_v)


def _rotate_half_cols(w):
    half = w.shape[-1] // 2
    return jnp.concatenate([-w[..., half:], w[..., :half]], axis=-1)


def _pad_cols(w, width):
    return jnp.pad(w, [(0, 0)] * (w.ndim - 1) + [(0, width - w.shape[-1])])


def _rope_tables(pos):
    half = MLA_ROPE // 2
    inv = ROPE_THETA ** (-jnp.arange(half, dtype=F32) / half)
    ang = pos.astype(F32)[:, None] * inv[None, :]
    c = jnp.cos(ang)
    s = jnp.sin(ang)
    return (_pad_cols(jnp.concatenate([c, c], axis=1), ROPE_PAD), _pad_cols(jnp.concatenate([s, s], axis=1), ROPE_PAD))


def _pick_tile(m, pref):
    t = min(m, pref)
    while m % t:
        t //= 2
    return t


def kernel(x_prompt, x_sample, mem_prompt, cache_ckv, cache_krope, cache_mem_k, cache_mem_v, state_pool, state_conv, page_table, w_in, w_out, norm_q_lat, w_qb, norm_kv_lat, w_uk, w_uv, sg_w, sg_b, sg_ln_g, sg_ln_b, pool_w, pool_scale, conv_w, conv_b, conv_ln_g, conv_ln_b, norm_mix, norm_ffn_a, w_ffn_a_in, w_ffn_a_out, norm_ffn_b, w_ffn_b_in, w_ffn_b_out, norm_cross, norm_mem, w_cq, w_ck, w_cv, w_co, norm_final):
    n_b, seq, d = x_prompt.shape
    n_db, t_new, _ = x_sample.shape
    depth = w_in.shape[0]
    n_pages = page_table.shape[1]
    page_size = cache_ckv.shape[2]
    past_len = n_pages * page_size
    mem_len = mem_prompt.shape[1]

    a_cols = Q_LORA + KV_LORA + MLA_ROPE
    b0, c0, d0 = a_cols, a_cols + 2 * GROUP_WIDTH, a_cols + 3 * GROUP_WIDTH
    w_kr = w_in[:, :, Q_LORA + KV_LORA:a_cols]
    w_in_r = jnp.concatenate([
        w_in[:, :, b0:c0], w_in[:, :, d0:], w_in[:, :, :Q_LORA], w_in[:, :, c0:d0],
        w_in[:, :, Q_LORA:Q_LORA + KV_LORA], _pad_cols(w_kr, ROPE_PAD), _pad_cols(_rotate_half_cols(w_kr), ROPE_PAD),
    ], axis=-1).astype(BF16)
    w_q_nope = w_qb[..., :MLA_NOPE].reshape(depth, Q_LORA, MLA_HEADS * MLA_NOPE).astype(BF16)
    w_q_r = w_qb[..., MLA_NOPE:]
    w_q_rope = _pad_cols(w_q_r, ROPE_PAD).reshape(depth, Q_LORA, MLA_HEADS * ROPE_PAD).astype(BF16)
    w_q_ropep = _pad_cols(_rotate_half_cols(w_q_r), ROPE_PAD).reshape(depth, Q_LORA, MLA_HEADS * ROPE_PAD).astype(BF16)
    w_ukt = jnp.transpose(w_uk, (0, 2, 3, 1)).astype(BF16)
    w_uvh = jnp.transpose(w_uv, (0, 2, 1, 3)).astype(BF16)
    w_out_b = w_out.astype(BF16)
    w_fa_in, w_fa_out = w_ffn_a_in.astype(BF16), w_ffn_a_out.astype(BF16)
    w_fb_in, w_fb_out = w_ffn_b_in.astype(BF16), w_ffn_b_out.astype(BF16)
    w_cq_b, w_ck_b, w_cv_b, w_co_b = (w.astype(BF16) for w in (w_cq, w_ck, w_cv, w_co))
    pool_w_b = pool_w.astype(BF16)
    row3 = lambda g: g.reshape(depth, 1, g.shape[-1])
    n_mix, n_fa, n_fb, n_cr, n_mem = (row3(g) for g in (norm_mix, norm_ffn_a, norm_ffn_b, norm_cross, norm_mem))
    n_q, n_kv = row3(norm_q_lat), row3(norm_kv_lat)
    sg_g, sg_beta = row3(sg_ln_g), row3(sg_ln_b)
    p_scale, c_b, c_g, c_beta = row3(pool_scale), row3(conv_b), row3(conv_ln_g), row3(conv_ln_b)
    n_fin = norm_final.reshape(1, d)
    reps = CHUNK // t_new
    sg_w_p = sg_w
    sg_w_s = jnp.tile(sg_w[:, :, :t_new, :t_new], (1, 1, reps, reps))
    bias_rows = lambda b: jnp.repeat(jnp.transpose(b, (0, 2, 1)), SG_HEAD_DIM, axis=2)
    sg_b_p = bias_rows(sg_b)
    sg_b_s = jnp.tile(bias_rows(sg_b[:, :, :t_new]), (1, reps, 1))

    cos_p, sin_p = _rope_tables(jnp.tile(jnp.arange(seq), n_b))
    cos_s, sin_s = _rope_tables(jnp.tile(past_len + jnp.arange(t_new), n_db))

    mp = n_b * seq
    ms = n_db * t_new
    xp = x_prompt.reshape(mp, d)
    xs = x_sample.reshape(ms, d)

    mem_k, mem_v = _mem_kv(mem_prompt.reshape(n_b * mem_len, d), n_mem, w_ck_b, w_cv_b, tm=_pick_tile(n_b * mem_len, 512))
    mem_k = mem_k.reshape(depth, n_b, mem_len, MEM_WIDTH)
    mem_v = mem_v.reshape(depth, n_b, mem_len, MEM_WIDTH)

    tm_p = _pick_tile(mp, 512)
    tm_s = _pick_tile(ms, 512)
    tf = 512
    bs_pc = _pick_tile(n_db, 16)
    bs_cr = _pick_tile(n_db, 8)
    ckv_p, kr_p, pool_p, conv_p = [], [], [], []
    ckv_s, kr_s, sgv_s, pool_s, conv_s = [], [], [], [], []
    for l in range(depth):
        last = l == depth - 1
        xp = _ffn(xp, n_fa, w_fa_in, w_fa_out, n_fin, l, final_norm=False, tm=tm_p, tf=tf)
        h = _rms_proj(xp, n_mix, w_in_r, l, tm=tm_p, tn=896)
        q_full, k_full, ckv, kr = _mla_proj(h, cos_p, sin_p, n_q, n_kv, w_q_nope, w_q_rope, w_q_ropep, w_ukt, l, tm=tm_p)
        o_lat = _mla_prompt(q_full, k_full, n_b, tq=_pick_tile(seq, 512))
        (y_b,) = _spatial_gate(h, sg_w_p, sg_b_p, sg_g, sg_beta, l, period=CHUNK, with_v=False, tm=tm_p)
        y_c, y_d, pst, cst = _pool_conv_prompt(h, n_b, pool_w_b, p_scale, conv_w, c_b, c_g, c_beta, l,
                                               tl=_pick_tile(seq, 256), rb=128)
        xp = _mix_out(xp, o_lat, y_b, y_c, y_d, w_uvh, w_out_b, l, tm=tm_p)
        xp = _cross_prompt(xp, n_cr, w_cq_b, w_co_b, mem_k, mem_v, l, tq=_pick_tile(seq, 512))
        xp = _ffn(xp, n_fb, w_fb_in, w_fb_out, n_fin, l, final_norm=last, tm=tm_p, tf=tf)
        ckv_p.append(ckv); kr_p.append(kr); pool_p.append(pst); conv_p.append(cst)
        xs = _ffn(xs, n_fa, w_fa_in, w_fa_out, n_fin, l, final_norm=False, tm=tm_s, tf=tf)
        h = _rms_proj(xs, n_mix, w_in_r, l, tm=tm_s, tn=896)
        q_full, k_full, ckv, kr = _mla_proj(h, cos_s, sin_s, n_q, n_kv, w_q_nope, w_q_rope, w_q_ropep, w_ukt, l, tm=tm_s)
        o_lat = _mla_sample(page_table, q_full.reshape(n_db, t_new * MLA_HEADS, QK_WIDTH),
                            k_full.reshape(n_db, t_new, QK_WIDTH), cache_ckv, cache_krope, l,
                            chunk=_pick_tile(past_len, 1024))
        o_lat = o_lat.reshape(ms, MLA_HEADS * KV_LORA)
        y_b, v_b = _spatial_gate(h, sg_w_s, sg_b_s, sg_g, sg_beta, l, period=t_new, with_v=True, tm=tm_s)
        y_c, y_d, pst, cst = _pool_conv_sample(h, state_pool, state_conv, pool_w_b, p_scale, conv_w, c_b, c_g, c_beta,
                                               l, bs=bs_pc, t_new=t_new, past_len=past_len)
        xs = _mix_out(xs, o_lat, y_b, y_c, y_d, w_uvh, w_out_b, l, tm=tm_s)
        xs = _cross_sample(xs, n_cr, w_cq_b, w_co_b, cache_mem_k, cache_mem_v, l, bs=bs_cr, t_new=t_new)
        xs = _ffn(xs, n_fb, w_fb_in, w_fb_out, n_fin, l, final_norm=last, tm=tm_s, tf=tf)
        ckv_s.append(ckv); kr_s.append(kr); sgv_s.append(v_b); pool_s.append(pst); conv_s.append(cst)

    stack_p = lambda xs_, w: jnp.stack(xs_).reshape(depth, n_b, seq, w)
    stack_s = lambda xs_, w: jnp.stack(xs_).reshape(depth, n_db, t_new, w)
    return (xp.reshape(n_b, seq, d), xs.reshape(n_db, t_new, d),
            stack_p(ckv_p, KV_LORA), stack_p(kr_p, MLA_ROPE), mem_k, mem_v, jnp.stack(pool_p), jnp.stack(conv_p),
            stack_s(ckv_s, KV_LORA), stack_s(kr_s, MLA_ROPE), stack_s(sgv_s, GROUP_WIDTH),
            jnp.stack(pool_s), jnp.stack(conv_s))
```

```python
import functools

import jax
import jax.numpy as jnp
from jax import lax
from jax.experimental import pallas as pl
from jax.experimental.pallas import tpu as pltpu

F32 = jnp.float32
BF16 = jnp.bfloat16

GROUP_WIDTH = 512
MLA_HEADS = 4
MLA_NOPE = 128
MLA_ROPE = 64
KV_LORA = 256
Q_LORA = 512
MLA_SCALE = (MLA_NOPE + MLA_ROPE) ** -0.5
ROPE_THETA = 10000.0
SG_HEADS = 4
SG_HEAD_DIM = 128
CHUNK = 128
POOL_WINDOWS = (2, 4, 8, 16)
POOL_GROUP = 128
POOL_STATE = 15
CONV_WIDTH = 31
CONV_STATE = 30
MEM_HEADS = 4
MEM_HEAD_DIM = 128
MEM_WIDTH = 512
EPS = 1e-6
NEG = -1e30

LANES = 128
SUBLANES = 8
V7X_VMEM_BYTES = 64 * 2**20

ROPE_PAD = LANES
COL_B = 0
COL_D = 1024
COL_QC = 2048
COL_C = 2560
COL_CKV = 3072
COL_KR = 3328
COL_KRP = 3456
H_COLS = 3584
QK_WIDTH = KV_LORA + ROPE_PAD
HALO_POOL = 16
HALO_CONV = 32


def _params(semantics, vmem_bytes):
    return pltpu.CompilerParams(dimension_semantics=semantics,
                                vmem_limit_bytes=int(min(vmem_bytes, V7X_VMEM_BYTES - 8 * 2**20)))


def _rms(x, g):
    return x * lax.rsqrt(jnp.mean(x * x, axis=-1, keepdims=True) + EPS) * g


def _ln(x, g, b):
    mu = jnp.mean(x, axis=-1, keepdims=True)
    xc = x - mu
    return xc * lax.rsqrt(jnp.mean(xc * xc, axis=-1, keepdims=True) + EPS) * g + b


def _dot(a, b):
    return jnp.dot(a, b, preferred_element_type=F32)


def _dot_nt(a, b):
    return lax.dot_general(a, b, (((1,), (1,)), ((), ())), preferred_element_type=F32)


def _softmax(s):
    e = jnp.exp(s - jnp.max(s, axis=-1, keepdims=True))
    return e / jnp.sum(e, axis=-1, keepdims=True)


def _ffn_body(x_ref, g_ref, wg_ref, wu_ref, wo_ref, gf_ref, o_ref, xn_ref, *, final_norm):
    j = pl.program_id(1)

    @pl.when(j == 0)
    def _():
        x = x_ref[...]
        xn_ref[...] = _rms(x, g_ref[...]).astype(BF16)
        o_ref[...] = x

    xn = xn_ref[...]
    hg = _dot(xn, wg_ref[...])
    hu = _dot(xn, wu_ref[...])
    act = (hg * jax.nn.sigmoid(hg) * (0.5 * hu)).astype(BF16)
    o_ref[...] += _dot(act, wo_ref[...])

    if final_norm:
        @pl.when(j == pl.num_programs(1) - 1)
        def _():
            o_ref[...] = _rms(o_ref[...], gf_ref[...])


def _ffn(x, gain, w_in, w_out, g_final, layer, *, final_norm, tm, tf):
    m, d = x.shape
    f = w_out.shape[1]
    nj = f // tf
    vmem = 2 * 2 * tm * d * 4 + tm * d * 2 + 2 * 3 * d * tf * 2 + tm * d * 4 + 3 * tm * tf * 4
    return pl.pallas_call(
        functools.partial(_ffn_body, final_norm=final_norm),
        out_shape=jax.ShapeDtypeStruct((m, d), F32),
        grid=(m // tm, nj),
        in_specs=[
            pl.BlockSpec((tm, d), lambda i, j: (i, 0)),
            pl.BlockSpec((None, 1, d), lambda i, j: (layer, 0, 0)),
            pl.BlockSpec((None, d, tf), lambda i, j: (layer, 0, j)),
            pl.BlockSpec((None, d, tf), lambda i, j: (layer, 0, j + nj)),
            pl.BlockSpec((None, tf, d), lambda i, j: (layer, j, 0)),
            pl.BlockSpec((1, d), lambda i, j: (0, 0)),
        ],
        out_specs=pl.BlockSpec((tm, d), lambda i, j: (i, 0)),
        scratch_shapes=[pltpu.VMEM((tm, d), BF16)],
        compiler_params=_params(("parallel", "arbitrary"), vmem + 4 * 2**20),
        name="ffn",
    )(x, gain, w_in, w_in, w_out, g_final)


def _proj_body(x_ref, g_ref, w_ref, o_ref, xn_ref):
    @pl.when(pl.program_id(1) == 0)
    def _():
        xn_ref[...] = _rms(x_ref[...], g_ref[...]).astype(BF16)

    o_ref[...] = _dot(xn_ref[...], w_ref[...])


def _rms_proj(x, gain, w, layer, *, tm, tn):
    m, d = x.shape
    n = w.shape[2]
    vmem = 2 * tm * d * 4 + tm * d * 2 + 2 * d * tn * 2 + 3 * tm * tn * 4
    return pl.pallas_call(
        _proj_body,
        out_shape=jax.ShapeDtypeStruct((m, n), F32),
        grid=(m // tm, n // tn),
        in_specs=[
            pl.BlockSpec((tm, d), lambda i, j: (i, 0)),
            pl.BlockSpec((None, 1, d), lambda i, j: (layer, 0, 0)),
            pl.BlockSpec((None, d, tn), lambda i, j: (layer, 0, j)),
        ],
        out_specs=pl.BlockSpec((tm, tn), lambda i, j: (i, j)),
        scratch_shapes=[pltpu.VMEM((tm, d), BF16)],
        compiler_params=_params(("parallel", "arbitrary"), vmem + 4 * 2**20),
        name="mix_in_proj",
    )(x, gain, w)


def _memkv_body(x_ref, g_ref, wk_ref, wv_ref, k_ref, v_ref):
    xn = _rms(x_ref[...], g_ref[...]).astype(BF16)
    k_ref[...] = _dot(xn, wk_ref[...])
    v_ref[...] = _dot(xn, wv_ref[...])


def _mem_kv(mem, gain, w_ck, w_cv, *, tm):
    m, d = mem.shape
    depth, _, n = w_ck.shape
    vmem = 2 * tm * d * 4 + tm * d * 2 + 2 * 2 * d * n * 2 + 2 * 2 * tm * n * 4
    out = jax.ShapeDtypeStruct((depth, m, n), F32)
    return pl.pallas_call(
        _memkv_body,
        out_shape=(out, out),
        grid=(depth, m // tm),
        in_specs=[
            pl.BlockSpec((tm, d), lambda l, i: (i, 0)),
            pl.BlockSpec((None, 1, d), lambda l, i: (l, 0, 0)),
            pl.BlockSpec((None, d, n), lambda l, i: (l, 0, 0)),
            pl.BlockSpec((None, d, n), lambda l, i: (l, 0, 0)),
        ],
        out_specs=(pl.BlockSpec((None, tm, n), lambda l, i: (l, i, 0)),
                   pl.BlockSpec((None, tm, n), lambda l, i: (l, i, 0))),
        compiler_params=_params(("parallel", "parallel"), vmem + 4 * 2**20),
        name="mem_kv",
    )(mem, gain, w_ck, w_cv)


def _mla_proj_body(qc_ref, ckv_ref, kr_ref, krp_ref, cos_ref, sin_ref, nq_ref, nkv_ref, wn_ref, wr_ref, wrp_ref,
                   wuk_ref, q_ref, k_ref, ckv_out_ref, kr_out_ref):
    cos = cos_ref[...]
    sin = sin_ref[...]
    qn = _rms(qc_ref[...], nq_ref[...]).astype(BF16)
    q_nope = _dot(qn, wn_ref[...]).astype(BF16)
    q_r = _dot(qn, wr_ref[...])
    q_rp = _dot(qn, wrp_ref[...])
    for h in range(MLA_HEADS):
        q_lat = _dot(q_nope[:, h * MLA_NOPE:(h + 1) * MLA_NOPE], wuk_ref[h])
        q_ref[:, h * QK_WIDTH:h * QK_WIDTH + KV_LORA] = (q_lat * MLA_SCALE).astype(BF16)
        sl = slice(h * ROPE_PAD, (h + 1) * ROPE_PAD)
        rot = q_r[:, sl] * cos + q_rp[:, sl] * sin
        q_ref[:, h * QK_WIDTH + KV_LORA:(h + 1) * QK_WIDTH] = (rot * MLA_SCALE).astype(BF16)
    ckv_n = _rms(ckv_ref[...], nkv_ref[...])
    ckv_out_ref[...] = ckv_n
    k_ref[:, :KV_LORA] = ckv_n.astype(BF16)
    k_rot = kr_ref[...] * cos + krp_ref[...] * sin
    kr_out_ref[...] = k_rot[:, :MLA_ROPE]
    k_ref[:, KV_LORA:] = k_rot.astype(BF16)


def _mla_proj(h, cos, sin, norm_q, norm_kv, w_nope, w_rope, w_ropep, w_ukt, layer, *, tm):
    m = h.shape[0]
    qw = MLA_HEADS * QK_WIDTH
    const = lambda *shape: pl.BlockSpec((None,) + shape, lambda i: (layer,) + (0,) * len(shape))
    vmem = 2 * tm * (Q_LORA + KV_LORA + 4 * ROPE_PAD) * 4 + 2 * tm * (qw + QK_WIDTH) * 2 \
        + 2 * tm * (KV_LORA + LANES) * 4 + 2 * (3 * Q_LORA * 512 + 4 * 128 * 256) * 2 + 8 * tm * 512 * 4
    return pl.pallas_call(
        _mla_proj_body,
        out_shape=(jax.ShapeDtypeStruct((m, qw), BF16), jax.ShapeDtypeStruct((m, QK_WIDTH), BF16),
                   jax.ShapeDtypeStruct((m, KV_LORA), F32), jax.ShapeDtypeStruct((m, MLA_ROPE), F32)),
        grid=(m // tm,),
        in_specs=[
            pl.BlockSpec((tm, Q_LORA), lambda i: (i, COL_QC // Q_LORA)),
            pl.BlockSpec((tm, KV_LORA), lambda i: (i, COL_CKV // KV_LORA)),
            pl.BlockSpec((tm, ROPE_PAD), lambda i: (i, COL_KR // ROPE_PAD)),
            pl.BlockSpec((tm, ROPE_PAD), lambda i: (i, COL_KRP // ROPE_PAD)),
            pl.BlockSpec((tm, ROPE_PAD), lambda i: (i, 0)),
            pl.BlockSpec((tm, ROPE_PAD), lambda i: (i, 0)),
            const(1, Q_LORA), const(1, KV_LORA),
            const(Q_LORA, MLA_HEADS * MLA_NOPE), const(Q_LORA, MLA_HEADS * ROPE_PAD),
            const(Q_LORA, MLA_HEADS * ROPE_PAD), const(MLA_HEADS, MLA_NOPE, KV_LORA),
        ],
        out_specs=(pl.BlockSpec((tm, qw), lambda i: (i, 0)), pl.BlockSpec((tm, QK_WIDTH), lambda i: (i, 0)),
                   pl.BlockSpec((tm, KV_LORA), lambda i: (i, 0)), pl.BlockSpec((tm, MLA_ROPE), lambda i: (i, 0))),
        compiler_params=_params(("parallel",), vmem + 4 * 2**20),
        name="mla_proj",
    )(h, h, h, h, cos, sin, norm_q, norm_kv, w_nope, w_rope, w_ropep, w_ukt)


def _mla_prompt_body(q_ref, k_ref, o_ref, m_ref, l_ref, acc_ref, *, tq):
    qi = pl.program_id(1)
    ki = pl.program_id(2)

    @pl.when(ki == 0)
    def _():
        m_ref[...] = jnp.full(m_ref.shape, NEG, F32)
        l_ref[...] = jnp.zeros(l_ref.shape, F32)
        acc_ref[...] = jnp.zeros(acc_ref.shape, F32)

    @pl.when(ki <= qi)
    def _():
        k = k_ref[...]
        v = k[:, :KV_LORA]
        q_pos = qi * tq + lax.broadcasted_iota(jnp.int32, (tq, tq), 0)
        k_pos = ki * tq + lax.broadcasted_iota(jnp.int32, (tq, tq), 1)
        allowed = k_pos <= q_pos
        for h in range(MLA_HEADS):
            s = _dot_nt(q_ref[:, h * QK_WIDTH:(h + 1) * QK_WIDTH], k)
            s = jnp.where(allowed, s, NEG)
            m_prev = m_ref[h][:, :1]
            l_prev = l_ref[h][:, :1]
            m_new = jnp.maximum(m_prev, jnp.max(s, axis=-1, keepdims=True))
            alpha = jnp.exp(m_prev - m_new)
            p = jnp.exp(s - m_new)
            l_new = alpha * l_prev + jnp.sum(p, axis=-1, keepdims=True)
            acc_ref[h] = alpha * acc_ref[h] + _dot(p.astype(BF16), v)
            m_ref[h] = jnp.broadcast_to(m_new, (tq, LANES))
            l_ref[h] = jnp.broadcast_to(l_new, (tq, LANES))

    @pl.when(ki == qi)
    def _():
        for h in range(MLA_HEADS):
            o_ref[:, h * KV_LORA:(h + 1) * KV_LORA] = (acc_ref[h] / l_ref[h][:, :1]).astype(BF16)


def _mla_prompt(q_full, k_full, n_batch, *, tq):
    m = q_full.shape[0]
    nq = m // n_batch // tq
    qw = MLA_HEADS * QK_WIDTH
    ow = MLA_HEADS * KV_LORA
    vmem = 2 * tq * qw * 2 + 2 * tq * QK_WIDTH * 2 + 2 * tq * ow * 2 \
        + MLA_HEADS * tq * (2 * LANES + KV_LORA) * 4 + 6 * tq * tq * 4
    return pl.pallas_call(
        functools.partial(_mla_prompt_body, tq=tq),
        out_shape=jax.ShapeDtypeStruct((m, ow), BF16),
        grid=(n_batch, nq, nq),
        in_specs=[
            pl.BlockSpec((tq, qw), lambda b, qi, ki: (b * nq + qi, 0)),
            pl.BlockSpec((tq, QK_WIDTH), lambda b, qi, ki: (b * nq + jnp.minimum(ki, qi), 0)),
        ],
        out_specs=pl.BlockSpec((tq, ow), lambda b, qi, ki: (b * nq + qi, 0)),
        scratch_shapes=[pltpu.VMEM((MLA_HEADS, tq, LANES), F32), pltpu.VMEM((MLA_HEADS, tq, LANES), F32),
                        pltpu.VMEM((MLA_HEADS, tq, KV_LORA), F32)],
        compiler_params=_params(("parallel", "parallel", "arbitrary"), vmem + 4 * 2**20),
        name="mla_prompt_attn",
    )(q_full, k_full)


def _ckv_page_copy(ckv_hbm, ckv_buf, sem, layer, page, slot, j, page_size):
    return pltpu.make_async_copy(ckv_hbm.at[layer, page], ckv_buf.at[slot, pl.ds(j * page_size, page_size)],
                                 sem.at[slot])


def _krt_page_copy(krt_hbm, krt_buf, sem, layer, page, slot, j, page_size):
    return pltpu.make_async_copy(krt_hbm.at[layer, page], krt_buf.at[slot, :, pl.ds(j * page_size, page_size)],
                                 sem.at[slot])


def _mla_sample_body(pt_ref, q_ref, kn_ref, ckv_hbm, krt_hbm, o_ref, ckv_buf, krt_buf, kbf_ref, sem_c, sem_r, *,
                     layer, n_pages, page_size, chunk, t_new):
    n = pl.program_id(0)
    n_samples = pl.num_programs(0)
    past = n_pages * page_size

    def start_fetch(sample, slot):
        for j in range(n_pages):
            page = pt_ref[sample, j]
            _ckv_page_copy(ckv_hbm, ckv_buf, sem_c, layer, page, slot, j, page_size).start()
            _krt_page_copy(krt_hbm, krt_buf, sem_r, layer, page, slot, j, page_size).start()

    def wait_fetch(slot):
        for j in range(n_pages):
            _ckv_page_copy(ckv_hbm, ckv_buf, sem_c, layer, 0, slot, j, page_size).wait()
            _krt_page_copy(krt_hbm, krt_buf, sem_r, layer, 0, slot, j, page_size).wait()

    slot = lax.rem(n, 2)

    @pl.when(n == 0)
    def _():
        start_fetch(0, 0)

    @pl.when(n + 1 < n_samples)
    def _():
        start_fetch(n + 1, 1 - slot)

    wait_fetch(slot)

    q = q_ref[...]
    q_lat = q[:, :KV_LORA]
    q_rope = q[:, KV_LORA:KV_LORA + MLA_ROPE]
    rows = q.shape[0]
    for c in range(past // chunk):
        sl = pl.ds(c * chunk, chunk)
        kbf_ref[sl, :] = ckv_buf[slot, sl, :].astype(BF16)
    kbf = kbf_ref[...]
    s_past = _dot_nt(q_lat, kbf) + _dot(q_rope, krt_buf[slot].astype(BF16))

    kn = kn_ref[...]
    s_new = _dot_nt(q, kn)
    q_tok = lax.broadcasted_iota(jnp.int32, (rows, t_new), 0) // MLA_HEADS
    k_tok = lax.broadcasted_iota(jnp.int32, (rows, t_new), 1)
    s_new = jnp.where(k_tok <= q_tok, s_new, NEG)

    m = jnp.maximum(jnp.max(s_past, axis=-1, keepdims=True), jnp.max(s_new, axis=-1, keepdims=True))
    p_past = jnp.exp(s_past - m)
    p_new = jnp.exp(s_new - m)
    denom = jnp.sum(p_past, axis=-1, keepdims=True) + jnp.sum(p_new, axis=-1, keepdims=True)
    p_past = (p_past / denom).astype(BF16)
    p_new = (p_new / denom).astype(BF16)
    o = _dot(p_past, kbf) + _dot(p_new, kn[:, :KV_LORA])
    o_ref[...] = o.astype(BF16)


def _mla_sample(page_table, q3, kn3, cache_ckv, cache_krope_t, layer, *, chunk):
    n_samples, rows, _ = q3.shape
    t_new = kn3.shape[1]
    n_pages = page_table.shape[1]
    page_size = cache_ckv.shape[2]
    past = n_pages * page_size
    vmem = 2 * past * (KV_LORA + MLA_ROPE) * 4 + past * (KV_LORA + MLA_ROPE) * 2 + rows * past * 4 * 5 \
        + 2 * chunk * KV_LORA * 4
    grid_spec = pltpu.PrefetchScalarGridSpec(
        num_scalar_prefetch=1,
        grid=(n_samples,),
        in_specs=[
            pl.BlockSpec((None, rows, QK_WIDTH), lambda n, pt: (n, 0, 0)),
            pl.BlockSpec((None, t_new, QK_WIDTH), lambda n, pt: (n, 0, 0)),
            pl.BlockSpec(memory_space=pl.ANY),
            pl.BlockSpec(memory_space=pl.ANY),
        ],
        out_specs=pl.BlockSpec((None, rows, KV_LORA), lambda n, pt: (n, 0, 0)),
        scratch_shapes=[
            pltpu.VMEM((2, past, KV_LORA), F32),
            pltpu.VMEM((2, MLA_ROPE, past), F32),
            pltpu.VMEM((past, KV_LORA), BF16),
            pltpu.SemaphoreType.DMA((2,)),
            pltpu.SemaphoreType.DMA((2,)),
        ],
    )
    return pl.pallas_call(
        functools.partial(_mla_sample_body, layer=layer, n_pages=n_pages, page_size=page_size, chunk=chunk,
                          t_new=t_new),
        out_shape=jax.ShapeDtypeStruct((n_samples, rows, KV_LORA), BF16),
        grid_spec=grid_spec,
        compiler_params=_params(("arbitrary",), vmem + 4 * 2**20),
        name="mla_sample_attn",
    )(page_table, q3, kn3, cache_ckv, cache_krope_t)


def _sg_body(h_ref, w_ref, b_ref, g_ref, beta_ref, y_ref, *v_refs, period, tm):
    row = lax.broadcasted_iota(jnp.int32, (CHUNK, CHUNK), 0)
    col = lax.broadcasted_iota(jnp.int32, (CHUNK, CHUNK), 1)
    shift = period.bit_length() - 1
    keep = (col <= row) & ((row >> shift) == (col >> shift))
    w = [jnp.where(keep, w_ref[h], 0.0).astype(BF16) for h in range(SG_HEADS)]
    bias = b_ref[...]
    for c in range(tm // CHUNK):
        rows = pl.ds(c * CHUNK, CHUNK)
        a = jax.nn.gelu(h_ref[rows, :])
        u = a[:, :GROUP_WIDTH]
        v = _ln(a[:, GROUP_WIDTH:], g_ref[...], beta_ref[...])
        if v_refs:
            v_refs[0][rows, :] = v
        vb = v.astype(BF16)
        for h in range(SG_HEADS):
            sl = slice(h * SG_HEAD_DIM, (h + 1) * SG_HEAD_DIM)
            g = _dot(w[h], vb[:, sl]) + bias[:, sl]
            y_ref[rows, sl] = (u[:, sl] * g).astype(BF16)


def _spatial_gate(h, w, b_rows, ln_g, ln_b, layer, *, period, with_v, tm):
    m = h.shape[0]
    const = lambda *shape: pl.BlockSpec((None,) + shape, lambda i: (layer,) + (0,) * len(shape))
    out_shape = [jax.ShapeDtypeStruct((m, GROUP_WIDTH), BF16)]
    out_specs = [pl.BlockSpec((tm, GROUP_WIDTH), lambda i: (i, 0))]
    if with_v:
        out_shape.append(jax.ShapeDtypeStruct((m, GROUP_WIDTH), F32))
        out_specs.append(pl.BlockSpec((tm, GROUP_WIDTH), lambda i: (i, 0)))
    vmem = 2 * tm * 2 * GROUP_WIDTH * 4 + 2 * tm * GROUP_WIDTH * 6 + 16 * CHUNK * 2 * GROUP_WIDTH * 4
    return pl.pallas_call(
        functools.partial(_sg_body, period=period, tm=tm),
        out_shape=tuple(out_shape),
        grid=(m // tm,),
        in_specs=[
            pl.BlockSpec((tm, 2 * GROUP_WIDTH), lambda i: (i, COL_B // (2 * GROUP_WIDTH))),
            const(SG_HEADS, CHUNK, CHUNK), const(CHUNK, GROUP_WIDTH), const(1, GROUP_WIDTH), const(1, GROUP_WIDTH),
        ],
        out_specs=tuple(out_specs),
        compiler_params=_params(("parallel",), vmem + 4 * 2**20),
        name="spatial_gate",
    )(h, w, b_rows, ln_g, ln_b)


def _conv_tail(y, ln_g, ln_b):
    return jax.nn.silu(_ln(y, ln_g, ln_b))


def _pc_prompt_body(hc_ref, hd_ref, pw_ref, ps_ref, cw_ref, cb_ref, g_ref, beta_ref,
                    yc_ref, yd_ref, pool_out_ref, conv_out_ref, cext, zext, ybuf, *, tl, rb):
    li = pl.program_id(1)

    @pl.when(li == 0)
    def _():
        cext[0:HALO_POOL, :] = jnp.zeros((HALO_POOL, GROUP_WIDTH), F32)
        zext[0:HALO_CONV, :] = jnp.zeros((HALO_CONV, GROUP_WIDTH), F32)

    hc = hc_ref[...]
    cext[HALO_POOL:HALO_POOL + tl, :] = hc
    pos = li * tl + lax.broadcasted_iota(jnp.int32, (tl, 1), 0)
    for gi, win in enumerate(POOL_WINDOWS):
        sl = slice(gi * POOL_GROUP, (gi + 1) * POOL_GROUP)
        acc = hc[:, sl]
        for k in range(1, win):
            acc = acc + cext[pl.ds(HALO_POOL - k, tl), sl]
        cnt = jnp.minimum(pos + 1, win).astype(F32)
        pooled = acc / cnt - hc[:, sl]
        y = _dot(pooled.astype(BF16), pw_ref[gi]) * ps_ref[:, sl]
        yc_ref[:, sl] = y.astype(BF16)

    hd = hd_ref[...]
    zext[HALO_CONV:HALO_CONV + tl, :] = hd[:, :GROUP_WIDTH] * jax.nn.sigmoid(hd[:, GROUP_WIDTH:])
    first = HALO_CONV - CONV_STATE
    for r0 in range(0, tl, rb):
        for cg in range(GROUP_WIDTH // LANES):
            sl = slice(cg * LANES, (cg + 1) * LANES)
            acc = jnp.broadcast_to(cb_ref[:, sl], (rb, LANES))
            for k in range(CONV_WIDTH):
                acc = acc + cw_ref[k:k + 1, sl] * zext[pl.ds(first + r0 + k, rb), sl]
            ybuf[r0:r0 + rb, sl] = acc
    yd_ref[...] = _conv_tail(ybuf[...], g_ref[...], beta_ref[...]).astype(BF16)

    @pl.when(li == pl.num_programs(1) - 1)
    def _():
        pool_out_ref[...] = cext[pl.ds(HALO_POOL + tl - POOL_STATE, POOL_STATE), :]
        conv_out_ref[...] = zext[pl.ds(HALO_CONV + tl - CONV_STATE, CONV_STATE), :]

    cext[0:HALO_POOL, :] = cext[tl:tl + HALO_POOL, :]
    zext[0:HALO_CONV, :] = zext[tl:tl + HALO_CONV, :]


def _pool_conv_prompt(h, n_batch, pool_w, pool_scale, conv_w, conv_b, ln_g, ln_b, layer, *, tl, rb):
    m = h.shape[0]
    nl = m // n_batch // tl
    const = lambda *shape: pl.BlockSpec((None,) + shape, lambda b, l: (layer,) + (0,) * len(shape))
    y = jax.ShapeDtypeStruct((m, GROUP_WIDTH), BF16)
    vmem = 2 * tl * 3 * GROUP_WIDTH * 4 + 2 * 2 * tl * GROUP_WIDTH * 2 + 3 * (tl + HALO_CONV) * GROUP_WIDTH * 4 \
        + 8 * tl * GROUP_WIDTH * 4
    return pl.pallas_call(
        functools.partial(_pc_prompt_body, tl=tl, rb=rb),
        out_shape=(y, y, jax.ShapeDtypeStruct((n_batch, POOL_STATE, GROUP_WIDTH), F32),
                   jax.ShapeDtypeStruct((n_batch, CONV_STATE, GROUP_WIDTH), F32)),
        grid=(n_batch, nl),
        in_specs=[
            pl.BlockSpec((tl, GROUP_WIDTH), lambda b, l: (b * nl + l, COL_C // GROUP_WIDTH)),
            pl.BlockSpec((tl, 2 * GROUP_WIDTH), lambda b, l: (b * nl + l, COL_D // (2 * GROUP_WIDTH))),
            const(len(POOL_WINDOWS), POOL_GROUP, POOL_GROUP), const(1, GROUP_WIDTH),
            const(CONV_WIDTH, GROUP_WIDTH), const(1, GROUP_WIDTH), const(1, GROUP_WIDTH), const(1, GROUP_WIDTH),
        ],
        out_specs=(pl.BlockSpec((tl, GROUP_WIDTH), lambda b, l: (b * nl + l, 0)),
                   pl.BlockSpec((tl, GROUP_WIDTH), lambda b, l: (b * nl + l, 0)),
                   pl.BlockSpec((None, POOL_STATE, GROUP_WIDTH), lambda b, l: (b, 0, 0)),
                   pl.BlockSpec((None, CONV_STATE, GROUP_WIDTH), lambda b, l: (b, 0, 0))),
        scratch_shapes=[pltpu.VMEM((HALO_POOL + tl, GROUP_WIDTH), F32), pltpu.VMEM((HALO_CONV + tl, GROUP_WIDTH), F32),
                        pltpu.VMEM((tl, GROUP_WIDTH), F32)],
        compiler_params=_params(("parallel", "arbitrary"), vmem + 4 * 2**20),
        name="pool_conv_prompt",
    )(h, h, pool_w, pool_scale, conv_w, conv_b, ln_g, ln_b)


def _pc_sample_body(hc_ref, hd_ref, sp_ref, sc_ref, pw_ref, ps_ref, cw_ref, cb_ref, g_ref, beta_ref,
                    yc_ref, yd_ref, pool_out_ref, conv_out_ref, cext, zext, ybuf, *, bs, t_new, past_len):
    rows = bs * t_new
    hc = hc_ref[...]
    cext[:, 0:POOL_STATE, :] = sp_ref[...]
    cext[:, POOL_STATE:POOL_STATE + t_new, :] = hc.reshape(bs, t_new, GROUP_WIDTH)
    pos = past_len + lax.broadcasted_iota(jnp.int32, (1, t_new, 1), 1)
    for gi, win in enumerate(POOL_WINDOWS):
        sl = slice(gi * POOL_GROUP, (gi + 1) * POOL_GROUP)
        acc = cext[:, pl.ds(POOL_STATE, t_new), sl]
        for k in range(1, win):
            acc = acc + cext[:, pl.ds(POOL_STATE - k, t_new), sl]
        cnt = jnp.minimum(pos + 1, win).astype(F32)
        pooled = (acc / cnt).reshape(rows, POOL_GROUP) - hc[:, sl]
        y = _dot(pooled.astype(BF16), pw_ref[gi]) * ps_ref[:, sl]
        yc_ref[:, sl] = y.astype(BF16)
    pool_out_ref[...] = cext[:, pl.ds(t_new, POOL_STATE), :]

    hd = hd_ref[...]
    z = hd[:, :GROUP_WIDTH] * jax.nn.sigmoid(hd[:, GROUP_WIDTH:])
    zext[:, 0:CONV_STATE, :] = sc_ref[...]
    zext[:, CONV_STATE:CONV_STATE + t_new, :] = z.reshape(bs, t_new, GROUP_WIDTH)
    for cg in range(GROUP_WIDTH // LANES):
        sl = slice(cg * LANES, (cg + 1) * LANES)
        acc = jnp.broadcast_to(cb_ref[:, sl].reshape(1, 1, LANES), (bs, t_new, LANES))
        for k in range(CONV_WIDTH):
            acc = acc + cw_ref[k:k + 1, sl].reshape(1, 1, LANES) * zext[:, pl.ds(k, t_new), sl]
        ybuf[:, sl] = acc.reshape(rows, LANES)
    yd_ref[...] = _conv_tail(ybuf[...], g_ref[...], beta_ref[...]).astype(BF16)
    conv_out_ref[...] = zext[:, pl.ds(t_new, CONV_STATE), :]


def _pool_conv_sample(h, state_pool, state_conv, pool_w, pool_scale, conv_w, conv_b, ln_g, ln_b, layer, *,
                      bs, t_new, past_len):
    m = h.shape[0]
    n_samples = m // t_new
    rows = bs * t_new
    const = lambda *shape: pl.BlockSpec((None,) + shape, lambda i: (layer,) + (0,) * len(shape))
    y = jax.ShapeDtypeStruct((m, GROUP_WIDTH), BF16)
    pool_rows = POOL_STATE + t_new + 1
    conv_rows = CONV_STATE + t_new + 2
    vmem = 2 * rows * 3 * GROUP_WIDTH * 4 + 4 * bs * (16 + 32) * GROUP_WIDTH * 4 \
        + bs * (pool_rows + conv_rows) * GROUP_WIDTH * 4 + 8 * rows * GROUP_WIDTH * 4
    return pl.pallas_call(
        functools.partial(_pc_sample_body, bs=bs, t_new=t_new, past_len=past_len),
        out_shape=(y, y, jax.ShapeDtypeStruct((n_samples, POOL_STATE, GROUP_WIDTH), F32),
                   jax.ShapeDtypeStruct((n_samples, CONV_STATE, GROUP_WIDTH), F32)),
        grid=(n_samples // bs,),
        in_specs=[
            pl.BlockSpec((rows, GROUP_WIDTH), lambda i: (i, COL_C // GROUP_WIDTH)),
            pl.BlockSpec((rows, 2 * GROUP_WIDTH), lambda i: (i, COL_D // (2 * GROUP_WIDTH))),
            pl.BlockSpec((None, bs, POOL_STATE, GROUP_WIDTH), lambda i: (layer, i, 0, 0)),
            pl.BlockSpec((None, bs, CONV_STATE, GROUP_WIDTH), lambda i: (layer, i, 0, 0)),
            const(len(POOL_WINDOWS), POOL_GROUP, POOL_GROUP), const(1, GROUP_WIDTH),
            const(CONV_WIDTH, GROUP_WIDTH), const(1, GROUP_WIDTH), const(1, GROUP_WIDTH), const(1, GROUP_WIDTH),
        ],
        out_specs=(pl.BlockSpec((rows, GROUP_WIDTH), lambda i: (i, 0)),
                   pl.BlockSpec((rows, GROUP_WIDTH), lambda i: (i, 0)),
                   pl.BlockSpec((bs, POOL_STATE, GROUP_WIDTH), lambda i: (i, 0, 0)),
                   pl.BlockSpec((bs, CONV_STATE, GROUP_WIDTH), lambda i: (i, 0, 0))),
        scratch_shapes=[pltpu.VMEM((bs, pool_rows, GROUP_WIDTH), F32), pltpu.VMEM((bs, conv_rows, GROUP_WIDTH), F32),
                        pltpu.VMEM((rows, GROUP_WIDTH), F32)],
        compiler_params=_params(("parallel",), vmem + 4 * 2**20),
        name="pool_conv_sample",
    )(h, h, state_pool, state_conv, pool_w, pool_scale, conv_w, conv_b, ln_g, ln_b)


def _mix_out_body(x_ref, ol_ref, yb_ref, yc_ref, yd_ref, wuv_ref, wo_ref, o_ref):
    ol = ol_ref[...]
    parts = [_dot(ol[:, h * KV_LORA:(h + 1) * KV_LORA], wuv_ref[h]).astype(BF16) for h in range(MLA_HEADS)]
    y = jnp.concatenate(parts + [yb_ref[...], yc_ref[...], yd_ref[...]], axis=1)
    o_ref[...] = x_ref[...] + _dot(y, wo_ref[...])


def _mix_out(x, o_lat, y_b, y_c, y_d, w_uv, w_out, layer, *, tm):
    m, d = x.shape
    mw = w_out.shape[1]
    row = lambda w: pl.BlockSpec((tm, w), lambda i: (i, 0))
    vmem = 2 * 2 * tm * d * 4 + 2 * tm * (MLA_HEADS * KV_LORA + 3 * GROUP_WIDTH) * 2 + 2 * mw * d * 2 \
        + tm * mw * 2 + tm * d * 4
    return pl.pallas_call(
        _mix_out_body,
        out_shape=jax.ShapeDtypeStruct((m, d), F32),
        grid=(m // tm,),
        in_specs=[
            row(d), row(MLA_HEADS * KV_LORA), row(GROUP_WIDTH), row(GROUP_WIDTH), row(GROUP_WIDTH),
            pl.BlockSpec((None, MLA_HEADS, KV_LORA, 128), lambda i: (layer, 0, 0, 0)),
            pl.BlockSpec((None, mw, d), lambda i: (layer, 0, 0)),
        ],
        out_specs=row(d),
        compiler_params=_params(("parallel",), vmem + 4 * 2**20),
        name="mix_out_proj",
    )(x, o_lat, y_b, y_c, y_d, w_uv, w_out)


def _cross_prompt_body(x_ref, g_ref, wq_ref, wo_ref, mk_ref, mv_ref, o_ref):
    x = x_ref[...]
    xn = _rms(x, g_ref[...]).astype(BF16)
    q = (_dot(xn, wq_ref[...]) * MEM_HEAD_DIM ** -0.5).astype(BF16)
    k = mk_ref[...].astype(BF16)
    v = mv_ref[...].astype(BF16)
    heads = []
    for h in range(MEM_HEADS):
        sl = slice(h * MEM_HEAD_DIM, (h + 1) * MEM_HEAD_DIM)
        p = _softmax(_dot_nt(q[:, sl], k[:, sl]))
        heads.append(_dot(p.astype(BF16), v[:, sl]).astype(BF16))
    o_ref[...] = x + _dot(jnp.concatenate(heads, axis=1), wo_ref[...])


def _cross_prompt(x, gain, w_cq, w_co, mem_k, mem_v, layer, *, tq):
    m, d = x.shape
    n_batch, mem_len = mem_k.shape[1:3]
    nq = m // n_batch // tq
    const = lambda *shape: pl.BlockSpec((None,) + shape, lambda b, i: (layer,) + (0,) * len(shape))
    vmem = 2 * 2 * tq * d * 4 + 2 * 2 * d * MEM_WIDTH * 2 + 2 * 2 * mem_len * MEM_WIDTH * 4 \
        + tq * d * 6 + 8 * tq * MEM_WIDTH * 4
    return pl.pallas_call(
        _cross_prompt_body,
        out_shape=jax.ShapeDtypeStruct((m, d), F32),
        grid=(n_batch, nq),
        in_specs=[
            pl.BlockSpec((tq, d), lambda b, i: (b * nq + i, 0)),
            const(1, d), const(d, MEM_WIDTH), const(MEM_WIDTH, d),
            pl.BlockSpec((None, None, mem_len, MEM_WIDTH), lambda b, i: (layer, b, 0, 0)),
            pl.BlockSpec((None, None, mem_len, MEM_WIDTH), lambda b, i: (layer, b, 0, 0)),
        ],
        out_specs=pl.BlockSpec((tq, d), lambda b, i: (b * nq + i, 0)),
        compiler_params=_params(("parallel", "parallel"), vmem + 4 * 2**20),
        name="cross_attn_prompt",
    )(x, gain, w_cq, w_co, mem_k, mem_v)


def _cross_sample_body(x_ref, g_ref, wq_ref, wo_ref, mk_ref, mv_ref, o_ref, attn_ref, *, bs, t_new):
    x = x_ref[...]
    xn = _rms(x, g_ref[...]).astype(BF16)
    q = _dot(xn, wq_ref[...]) * MEM_HEAD_DIM ** -0.5
    rows = MEM_HEADS * t_new
    lane_head = lax.broadcasted_iota(jnp.int32, (rows, MEM_WIDTH), 1) // MEM_HEAD_DIM
    row_head = lax.broadcasted_iota(jnp.int32, (rows, MEM_WIDTH), 0) // t_new
    own = lane_head == row_head
    for s in range(bs):
        qs = q[s * t_new:(s + 1) * t_new, :]
        q_bd = jnp.where(own, jnp.concatenate([qs] * MEM_HEADS, axis=0), 0.0).astype(BF16)
        p = _softmax(_dot_nt(q_bd, mk_ref[s].astype(BF16)))
        r = jnp.where(own, _dot(p.astype(BF16), mv_ref[s].astype(BF16)), 0.0)
        o = r[0:t_new]
        for h in range(1, MEM_HEADS):
            o = o + r[h * t_new:(h + 1) * t_new]
        attn_ref[s * t_new:(s + 1) * t_new, :] = o
    o_ref[...] = x + _dot(attn_ref[...].astype(BF16), wo_ref[...])


def _cross_sample(x, gain, w_cq, w_co, mem_k, mem_v, layer, *, bs, t_new):
    m, d = x.shape
    mem_len = mem_k.shape[2]
    rows = bs * t_new
    const = lambda *shape: pl.BlockSpec((None,) + shape, lambda i: (layer,) + (0,) * len(shape))
    vmem = 2 * 2 * rows * d * 4 + 2 * 2 * d * MEM_WIDTH * 2 + 2 * 2 * bs * mem_len * MEM_WIDTH * 4 \
        + rows * d * 6 + 16 * mem_len * MEM_WIDTH * 4
    return pl.pallas_call(
        functools.partial(_cross_sample_body, bs=bs, t_new=t_new),
        out_shape=jax.ShapeDtypeStruct((m, d), F32),
        grid=(m // rows,),
        in_specs=[
            pl.BlockSpec((rows, d), lambda i: (i, 0)),
            const(1, d), const(d, MEM_WIDTH), const(MEM_WIDTH, d),
            pl.BlockSpec((None, bs, mem_len, MEM_WIDTH), lambda i: (layer, i, 0, 0)),
            pl.BlockSpec((None, bs, mem_len, MEM_WIDTH), lambda i: (layer, i, 0, 0)),
        ],
        out_specs=pl.BlockSpec((rows, d), lambda i: (i, 0)),
        scratch_shapes=[pltpu.VMEM((rows, MEM_WIDTH), F32)],
        compiler_params=_params(("parallel",), vmem + 4 * 2**20),
        name="cross_attn_sample",
    )(x, gain, w_cq, w_co, mem_k, mem_v)


def _rotate_half_cols(w):
    half = w.shape[-1] // 2
    return jnp.concatenate([-w[..., half:], w[..., :half]], axis=-1)


def _pad_cols(w, width):
    return jnp.pad(w, [(0, 0)] * (w.ndim - 1) + [(0, width - w.shape[-1])])


def _rope_tables(pos):
    half = MLA_ROPE // 2
    inv = ROPE_THETA ** (-jnp.arange(half, dtype=F32) / half)
    ang = pos.astype(F32)[:, None] * inv[None, :]
    c = jnp.cos(ang)
    s = jnp.sin(ang)
    return (_pad_cols(jnp.concatenate([c, c], axis=1), ROPE_PAD), _pad_cols(jnp.concatenate([s, s], axis=1), ROPE_PAD))


def _pick_tile(m, pref):
    t = min(m, pref)
    while m % t:
        t //= 2
    return t


def kernel(x_prompt, x_sample, mem_prompt, cache_ckv, cache_krope, cache_mem_k, cache_mem_v, state_pool, state_conv, page_table, w_in, w_out, norm_q_lat, w_qb, norm_kv_lat, w_uk, w_uv, sg_w, sg_b, sg_ln_g, sg_ln_b, pool_w, pool_scale, conv_w, conv_b, conv_ln_g, conv_ln_b, norm_mix, norm_ffn_a, w_ffn_a_in, w_ffn_a_out, norm_ffn_b, w_ffn_b_in, w_ffn_b_out, norm_cross, norm_mem, w_cq, w_ck, w_cv, w_co, norm_final):
    n_b, seq, d = x_prompt.shape
    n_db, t_new, _ = x_sample.shape
    depth = w_in.shape[0]
    n_pages = page_table.shape[1]
    page_size = cache_ckv.shape[2]
    past_len = n_pages * page_size
    mem_len = mem_prompt.shape[1]

    a_cols = Q_LORA + KV_LORA + MLA_ROPE
    b0, c0, d0 = a_cols, a_cols + 2 * GROUP_WIDTH, a_cols + 3 * GROUP_WIDTH
    w_kr = w_in[:, :, Q_LORA + KV_LORA:a_cols]
    w_in_r = jnp.concatenate([
        w_in[:, :, b0:c0], w_in[:, :, d0:], w_in[:, :, :Q_LORA], w_in[:, :, c0:d0],
        w_in[:, :, Q_LORA:Q_LORA + KV_LORA], _pad_cols(w_kr, ROPE_PAD), _pad_cols(_rotate_half_cols(w_kr), ROPE_PAD),
    ], axis=-1).astype(BF16)
    w_q_nope = w_qb[..., :MLA_NOPE].reshape(depth, Q_LORA, MLA_HEADS * MLA_NOPE).astype(BF16)
    w_q_r = w_qb[..., MLA_NOPE:]
    w_q_rope = _pad_cols(w_q_r, ROPE_PAD).reshape(depth, Q_LORA, MLA_HEADS * ROPE_PAD).astype(BF16)
    w_q_ropep = _pad_cols(_rotate_half_cols(w_q_r), ROPE_PAD).reshape(depth, Q_LORA, MLA_HEADS * ROPE_PAD).astype(BF16)
    w_ukt = jnp.transpose(w_uk, (0, 2, 3, 1)).astype(BF16)
    w_uvh = jnp.transpose(w_uv, (0, 2, 1, 3)).astype(BF16)
    w_out_b = w_out.astype(BF16)
    w_fa_in, w_fa_out = w_ffn_a_in.astype(BF16), w_ffn_a_out.astype(BF16)
    w_fb_in, w_fb_out = w_ffn_b_in.astype(BF16), w_ffn_b_out.astype(BF16)
    w_cq_b, w_ck_b, w_cv_b, w_co_b = (w.astype(BF16) for w in (w_cq, w_ck, w_cv, w_co))
    pool_w_b = pool_w.astype(BF16)
    row3 = lambda g: g.reshape(depth, 1, g.shape[-1])
    n_mix, n_fa, n_fb, n_cr, n_mem = (row3(g) for g in (norm_mix, norm_ffn_a, norm_ffn_b, norm_cross, norm_mem))
    n_q, n_kv = row3(norm_q_lat), row3(norm_kv_lat)
    sg_g, sg_beta = row3(sg_ln_g), row3(sg_ln_b)
    p_scale, c_b, c_g, c_beta = row3(pool_scale), row3(conv_b), row3(conv_ln_g), row3(conv_ln_b)
    n_fin = norm_final.reshape(1, d)
    reps = CHUNK // t_new
    sg_w_p = sg_w
    sg_w_s = jnp.tile(sg_w[:, :, :t_new, :t_new], (1, 1, reps, reps))
    bias_rows = lambda b: jnp.repeat(jnp.transpose(b, (0, 2, 1)), SG_HEAD_DIM, axis=2)
    sg_b_p = bias_rows(sg_b)
    sg_b_s = jnp.tile(bias_rows(sg_b[:, :, :t_new]), (1, reps, 1))

    cache_krope_t = jnp.swapaxes(cache_krope, 2, 3)

    cos_p, sin_p = _rope_tables(jnp.tile(jnp.arange(seq), n_b))
    cos_s, sin_s = _rope_tables(jnp.tile(past_len + jnp.arange(t_new), n_db))

    mp = n_b * seq
    ms = n_db * t_new
    xp = x_prompt.reshape(mp, d)
    xs = x_sample.reshape(ms, d)

    mem_k, mem_v = _mem_kv(mem_prompt.reshape(n_b * mem_len, d), n_mem, w_ck_b, w_cv_b, tm=_pick_tile(n_b * mem_len, 512))
    mem_k = mem_k.reshape(depth, n_b, mem_len, MEM_WIDTH)
    mem_v = mem_v.reshape(depth, n_b, mem_len, MEM_WIDTH)

    tm_p = _pick_tile(mp, 512)
    tm_s = _pick_tile(ms, 512)
    tmf_p = _pick_tile(mp, 1024)
    tmf_s = _pick_tile(ms, 1024)
    tf = 512
    bs_pc = _pick_tile(n_db, 16)
    bs_cr = _pick_tile(n_db, 8)
    ckv_p, kr_p, pool_p, conv_p = [], [], [], []
    ckv_s, kr_s, sgv_s, pool_s, conv_s = [], [], [], [], []
    for l in range(depth):
        last = l == depth - 1
        xp = _ffn(xp, n_fa, w_fa_in, w_fa_out, n_fin, l, final_norm=False, tm=tmf_p, tf=tf)
        h = _rms_proj(xp, n_mix, w_in_r, l, tm=_pick_tile(mp, 1024), tn=896)
        q_full, k_full, ckv, kr = _mla_proj(h, cos_p, sin_p, n_q, n_kv, w_q_nope, w_q_rope, w_q_ropep, w_ukt, l, tm=tm_p)
        o_lat = _mla_prompt(q_full, k_full, n_b, tq=_pick_tile(seq, 512))
        (y_b,) = _spatial_gate(h, sg_w_p, sg_b_p, sg_g, sg_beta, l, period=CHUNK, with_v=False, tm=tm_p)
        y_c, y_d, pst, cst = _pool_conv_prompt(h, n_b, pool_w_b, p_scale, conv_w, c_b, c_g, c_beta, l,
                                               tl=_pick_tile(seq, 256), rb=128)
        xp = _mix_out(xp, o_lat, y_b, y_c, y_d, w_uvh, w_out_b, l, tm=tm_p)
        xp = _cross_prompt(xp, n_cr, w_cq_b, w_co_b, mem_k, mem_v, l, tq=_pick_tile(seq, 512))
        xp = _ffn(xp, n_fb, w_fb_in, w_fb_out, n_fin, l, final_norm=last, tm=tmf_p, tf=tf)
        ckv_p.append(ckv); kr_p.append(kr); pool_p.append(pst); conv_p.append(cst)
        xs = _ffn(xs, n_fa, w_fa_in, w_fa_out, n_fin, l, final_norm=False, tm=tmf_s, tf=tf)
        h = _rms_proj(xs, n_mix, w_in_r, l, tm=_pick_tile(ms, 1024), tn=896)
        q_full, k_full, ckv, kr = _mla_proj(h, cos_s, sin_s, n_q, n_kv, w_q_nope, w_q_rope, w_q_ropep, w_ukt, l, tm=tm_s)
        o_lat = _mla_sample(page_table, q_full.reshape(n_db, t_new * MLA_HEADS, QK_WIDTH),
                            k_full.reshape(n_db, t_new, QK_WIDTH), cache_ckv, cache_krope_t, l,
                            chunk=_pick_tile(past_len, 1024))
        o_lat = o_lat.reshape(ms, MLA_HEADS * KV_LORA)
        y_b, v_b = _spatial_gate(h, sg_w_s, sg_b_s, sg_g, sg_beta, l, period=t_new, with_v=True, tm=tm_s)
        y_c, y_d, pst, cst = _pool_conv_sample(h, state_pool, state_conv, pool_w_b, p_scale, conv_w, c_b, c_g, c_beta,
                                               l, bs=bs_pc, t_new=t_new, past_len=past_len)
        xs = _mix_out(xs, o_lat, y_b, y_c, y_d, w_uvh, w_out_b, l, tm=tm_s)
        xs = _cross_sample(xs, n_cr, w_cq_b, w_co_b, cache_mem_k, cache_mem_v, l, bs=bs_cr, t_new=t_new)
        xs = _ffn(xs, n_fb, w_fb_in, w_fb_out, n_fin, l, final_norm=last, tm=tmf_s, tf=tf)
        ckv_s.append(ckv); kr_s.append(kr); sgv_s.append(v_b); pool_s.append(pst); conv_s.append(cst)

    stack_p = lambda xs_, w: jnp.stack(xs_).reshape(depth, n_b, seq, w)
    stack_s = lambda xs_, w: jnp.stack(xs_).reshape(depth, n_db, t_new, w)
    return (xp.reshape(n_b, seq, d), xs.reshape(n_db, t_new, d),
            stack_p(ckv_p, KV_LORA), stack_p(kr_p, MLA_ROPE), mem_k, mem_v, jnp.stack(pool_p), jnp.stack(conv_p),
            stack_s(ckv_s, KV_LORA), stack_s(kr_s, MLA_ROPE), stack_s(sgv_s, GROUP_WIDTH),
            jnp.stack(pool_s), jnp.stack(conv_s))
```

```python
import functools

import jax
import jax.numpy as jnp
from jax import lax
from jax.experimental import pallas as pl
from jax.experimental.pallas import tpu as pltpu

F32 = jnp.float32
BF16 = jnp.bfloat16

GROUP_WIDTH = 512
MLA_HEADS = 4
MLA_NOPE = 128
MLA_ROPE = 64
KV_LORA = 256
Q_LORA = 512
MLA_SCALE = (MLA_NOPE + MLA_ROPE) ** -0.5
ROPE_THETA = 10000.0
SG_HEADS = 4
SG_HEAD_DIM = 128
CHUNK = 128
POOL_WINDOWS = (2, 4, 8, 16)
POOL_GROUP = 128
POOL_STATE = 15
CONV_WIDTH = 31
CONV_STATE = 30
MEM_HEADS = 4
MEM_HEAD_DIM = 128
MEM_WIDTH = 512
EPS = 1e-6
NEG = -1e30

LANES = 128
SUBLANES = 8
V7X_VMEM_BYTES = 64 * 2**20

ROPE_PAD = LANES
COL_B = 0
COL_D = 1024
COL_QC = 2048
COL_C = 2560
COL_CKV = 3072
COL_KR = 3328
COL_KRP = 3456
H_COLS = 3584
QK_WIDTH = KV_LORA + ROPE_PAD
HALO_POOL = 16
HALO_CONV = 32


def _params(semantics, vmem_bytes):
    return pltpu.CompilerParams(dimension_semantics=semantics,
                                vmem_limit_bytes=int(min(vmem_bytes, V7X_VMEM_BYTES - 8 * 2**20)))


def _rms(x, g):
    return x * lax.rsqrt(jnp.mean(x * x, axis=-1, keepdims=True) + EPS) * g


def _ln(x, g, b):
    mu = jnp.mean(x, axis=-1, keepdims=True)
    xc = x - mu
    return xc * lax.rsqrt(jnp.mean(xc * xc, axis=-1, keepdims=True) + EPS) * g + b


def _dot(a, b):
    return jnp.dot(a, b, preferred_element_type=F32)


def _dot_nt(a, b):
    return lax.dot_general(a, b, (((1,), (1,)), ((), ())), preferred_element_type=F32)


def _softmax(s):
    e = jnp.exp(s - jnp.max(s, axis=-1, keepdims=True))
    return e / jnp.sum(e, axis=-1, keepdims=True)


def _ffn_body(x_ref, g_ref, wg_ref, wu_ref, wo_ref, gf_ref, o_ref, xn_ref, *, final_norm):
    j = pl.program_id(1)

    @pl.when(j == 0)
    def _():
        x = x_ref[...]
        xn_ref[...] = _rms(x, g_ref[...]).astype(BF16)
        o_ref[...] = x

    xn = xn_ref[...]
    hg = _dot(xn, wg_ref[...])
    hu = _dot(xn, wu_ref[...])
    act = (hg * jax.nn.sigmoid(hg) * (0.5 * hu)).astype(BF16)
    o_ref[...] += _dot(act, wo_ref[...])

    if final_norm:
        @pl.when(j == pl.num_programs(1) - 1)
        def _():
            o_ref[...] = _rms(o_ref[...], gf_ref[...])


def _ffn(x, gain, w_in, w_out, g_final, layer, *, final_norm, tm, tf):
    m, d = x.shape
    f = w_out.shape[1]
    nj = f // tf
    vmem = 2 * 2 * tm * d * 4 + tm * d * 2 + 2 * 3 * d * tf * 2 + tm * d * 4 + 3 * tm * tf * 4
    return pl.pallas_call(
        functools.partial(_ffn_body, final_norm=final_norm),
        out_shape=jax.ShapeDtypeStruct((m, d), F32),
        grid=(m // tm, nj),
        in_specs=[
            pl.BlockSpec((tm, d), lambda i, j: (i, 0)),
            pl.BlockSpec((None, 1, d), lambda i, j: (layer, 0, 0)),
            pl.BlockSpec((None, d, tf), lambda i, j: (layer, 0, j)),
            pl.BlockSpec((None, d, tf), lambda i, j: (layer, 0, j + nj)),
            pl.BlockSpec((None, tf, d), lambda i, j: (layer, j, 0)),
            pl.BlockSpec((1, d), lambda i, j: (0, 0)),
        ],
        out_specs=pl.BlockSpec((tm, d), lambda i, j: (i, 0)),
        scratch_shapes=[pltpu.VMEM((tm, d), BF16)],
        compiler_params=_params(("parallel", "arbitrary"), vmem + 4 * 2**20),
        name="ffn",
    )(x, gain, w_in, w_in, w_out, g_final)


def _proj_body(x_ref, g_ref, w_ref, o_ref, xn_ref):
    @pl.when(pl.program_id(1) == 0)
    def _():
        xn_ref[...] = _rms(x_ref[...], g_ref[...]).astype(BF16)

    o_ref[...] = _dot(xn_ref[...], w_ref[...])


def _rms_proj(x, gain, w, layer, *, tm, tn):
    m, d = x.shape
    n = w.shape[2]
    vmem = 2 * tm * d * 4 + tm * d * 2 + 2 * d * tn * 2 + 3 * tm * tn * 4
    return pl.pallas_call(
        _proj_body,
        out_shape=jax.ShapeDtypeStruct((m, n), F32),
        grid=(m // tm, n // tn),
        in_specs=[
            pl.BlockSpec((tm, d), lambda i, j: (i, 0)),
            pl.BlockSpec((None, 1, d), lambda i, j: (layer, 0, 0)),
            pl.BlockSpec((None, d, tn), lambda i, j: (layer, 0, j)),
        ],
        out_specs=pl.BlockSpec((tm, tn), lambda i, j: (i, j)),
        scratch_shapes=[pltpu.VMEM((tm, d), BF16)],
        compiler_params=_params(("parallel", "arbitrary"), vmem + 4 * 2**20),
        name="mix_in_proj",
    )(x, gain, w)


def _memkv_body(x_ref, g_ref, wk_ref, wv_ref, k_ref, v_ref):
    xn = _rms(x_ref[...], g_ref[...]).astype(BF16)
    k_ref[...] = _dot(xn, wk_ref[...])
    v_ref[...] = _dot(xn, wv_ref[...])


def _mem_kv(mem, gain, w_ck, w_cv, *, tm):
    m, d = mem.shape
    depth, _, n = w_ck.shape
    vmem = 2 * tm * d * 4 + tm * d * 2 + 2 * 2 * d * n * 2 + 2 * 2 * tm * n * 4
    out = jax.ShapeDtypeStruct((depth, m, n), F32)
    return pl.pallas_call(
        _memkv_body,
        out_shape=(out, out),
        grid=(depth, m // tm),
        in_specs=[
            pl.BlockSpec((tm, d), lambda l, i: (i, 0)),
            pl.BlockSpec((None, 1, d), lambda l, i: (l, 0, 0)),
            pl.BlockSpec((None, d, n), lambda l, i: (l, 0, 0)),
            pl.BlockSpec((None, d, n), lambda l, i: (l, 0, 0)),
        ],
        out_specs=(pl.BlockSpec((None, tm, n), lambda l, i: (l, i, 0)),
                   pl.BlockSpec((None, tm, n), lambda l, i: (l, i, 0))),
        compiler_params=_params(("parallel", "parallel"), vmem + 4 * 2**20),
        name="mem_kv",
    )(mem, gain, w_ck, w_cv)


def _mla_proj_body(qc_ref, ckv_ref, kr_ref, krp_ref, cos_ref, sin_ref, nq_ref, nkv_ref, wn_ref, wr_ref, wrp_ref,
                   wuk_ref, q_ref, k_ref, ckv_out_ref, kr_out_ref):
    cos = cos_ref[...]
    sin = sin_ref[...]
    qn = _rms(qc_ref[...], nq_ref[...]).astype(BF16)
    q_nope = _dot(qn, wn_ref[...]).astype(BF16)
    q_r = _dot(qn, wr_ref[...])
    q_rp = _dot(qn, wrp_ref[...])
    for h in range(MLA_HEADS):
        q_lat = _dot(q_nope[:, h * MLA_NOPE:(h + 1) * MLA_NOPE], wuk_ref[h])
        q_ref[:, h * QK_WIDTH:h * QK_WIDTH + KV_LORA] = (q_lat * MLA_SCALE).astype(BF16)
        sl = slice(h * ROPE_PAD, (h + 1) * ROPE_PAD)
        rot = q_r[:, sl] * cos + q_rp[:, sl] * sin
        q_ref[:, h * QK_WIDTH + KV_LORA:(h + 1) * QK_WIDTH] = (rot * MLA_SCALE).astype(BF16)
    ckv_n = _rms(ckv_ref[...], nkv_ref[...])
    ckv_out_ref[...] = ckv_n
    k_ref[:, :KV_LORA] = ckv_n.astype(BF16)
    k_rot = kr_ref[...] * cos + krp_ref[...] * sin
    kr_out_ref[...] = k_rot[:, :MLA_ROPE]
    k_ref[:, KV_LORA:] = k_rot.astype(BF16)


def _mla_proj(h, cos, sin, norm_q, norm_kv, w_nope, w_rope, w_ropep, w_ukt, layer, *, tm):
    m = h.shape[0]
    qw = MLA_HEADS * QK_WIDTH
    const = lambda *shape: pl.BlockSpec((None,) + shape, lambda i: (layer,) + (0,) * len(shape))
    vmem = 2 * tm * (Q_LORA + KV_LORA + 4 * ROPE_PAD) * 4 + 2 * tm * (qw + QK_WIDTH) * 2 \
        + 2 * tm * (KV_LORA + LANES) * 4 + 2 * (3 * Q_LORA * 512 + 4 * 128 * 256) * 2 + 8 * tm * 512 * 4
    return pl.pallas_call(
        _mla_proj_body,
        out_shape=(jax.ShapeDtypeStruct((m, qw), BF16), jax.ShapeDtypeStruct((m, QK_WIDTH), BF16),
                   jax.ShapeDtypeStruct((m, KV_LORA), F32), jax.ShapeDtypeStruct((m, MLA_ROPE), F32)),
        grid=(m // tm,),
        in_specs=[
            pl.BlockSpec((tm, Q_LORA), lambda i: (i, COL_QC // Q_LORA)),
            pl.BlockSpec((tm, KV_LORA), lambda i: (i, COL_CKV // KV_LORA)),
            pl.BlockSpec((tm, ROPE_PAD), lambda i: (i, COL_KR // ROPE_PAD)),
            pl.BlockSpec((tm, ROPE_PAD), lambda i: (i, COL_KRP // ROPE_PAD)),
            pl.BlockSpec((tm, ROPE_PAD), lambda i: (i, 0)),
            pl.BlockSpec((tm, ROPE_PAD), lambda i: (i, 0)),
            const(1, Q_LORA), const(1, KV_LORA),
            const(Q_LORA, MLA_HEADS * MLA_NOPE), const(Q_LORA, MLA_HEADS * ROPE_PAD),
            const(Q_LORA, MLA_HEADS * ROPE_PAD), const(MLA_HEADS, MLA_NOPE, KV_LORA),
        ],
        out_specs=(pl.BlockSpec((tm, qw), lambda i: (i, 0)), pl.BlockSpec((tm, QK_WIDTH), lambda i: (i, 0)),
                   pl.BlockSpec((tm, KV_LORA), lambda i: (i, 0)), pl.BlockSpec((tm, MLA_ROPE), lambda i: (i, 0))),
        compiler_params=_params(("parallel",), vmem + 4 * 2**20),
        name="mla_proj",
    )(h, h, h, h, cos, sin, norm_q, norm_kv, w_nope, w_rope, w_ropep, w_ukt)


def _mla_prompt_body(q_ref, k_ref, o_ref, m_ref, l_ref, acc_ref, *, tq):
    qi = pl.program_id(1)
    ki = pl.program_id(2)

    @pl.when(ki == 0)
    def _():
        m_ref[...] = jnp.full(m_ref.shape, NEG, F32)
        l_ref[...] = jnp.zeros(l_ref.shape, F32)
        acc_ref[...] = jnp.zeros(acc_ref.shape, F32)

    @pl.when(ki <= qi)
    def _():
        k = k_ref[...]
        v = k[:, :KV_LORA]
        q_pos = qi * tq + lax.broadcasted_iota(jnp.int32, (tq, tq), 0)
        k_pos = ki * tq + lax.broadcasted_iota(jnp.int32, (tq, tq), 1)
        allowed = k_pos <= q_pos
        for h in range(MLA_HEADS):
            s = _dot_nt(q_ref[:, h * QK_WIDTH:(h + 1) * QK_WIDTH], k)
            s = jnp.where(allowed, s, NEG)
            m_prev = m_ref[h][:, :1]
            l_prev = l_ref[h][:, :1]
            m_new = jnp.maximum(m_prev, jnp.max(s, axis=-1, keepdims=True))
            alpha = jnp.exp(m_prev - m_new)
            p = jnp.exp(s - m_new)
            l_new = alpha * l_prev + jnp.sum(p, axis=-1, keepdims=True)
            acc_ref[h] = alpha * acc_ref[h] + _dot(p.astype(BF16), v)
            m_ref[h] = jnp.broadcast_to(m_new, (tq, LANES))
            l_ref[h] = jnp.broadcast_to(l_new, (tq, LANES))

    @pl.when(ki == qi)
    def _():
        for h in range(MLA_HEADS):
            o_ref[:, h * KV_LORA:(h + 1) * KV_LORA] = (acc_ref[h] / l_ref[h][:, :1]).astype(BF16)


def _mla_prompt(q_full, k_full, n_batch, *, tq):
    m = q_full.shape[0]
    nq = m // n_batch // tq
    qw = MLA_HEADS * QK_WIDTH
    ow = MLA_HEADS * KV_LORA
    vmem = 2 * tq * qw * 2 + 2 * tq * QK_WIDTH * 2 + 2 * tq * ow * 2 \
        + MLA_HEADS * tq * (2 * LANES + KV_LORA) * 4 + 6 * tq * tq * 4
    return pl.pallas_call(
        functools.partial(_mla_prompt_body, tq=tq),
        out_shape=jax.ShapeDtypeStruct((m, ow), BF16),
        grid=(n_batch, nq, nq),
        in_specs=[
            pl.BlockSpec((tq, qw), lambda b, qi, ki: (b * nq + qi, 0)),
            pl.BlockSpec((tq, QK_WIDTH), lambda b, qi, ki: (b * nq + jnp.minimum(ki, qi), 0)),
        ],
        out_specs=pl.BlockSpec((tq, ow), lambda b, qi, ki: (b * nq + qi, 0)),
        scratch_shapes=[pltpu.VMEM((MLA_HEADS, tq, LANES), F32), pltpu.VMEM((MLA_HEADS, tq, LANES), F32),
                        pltpu.VMEM((MLA_HEADS, tq, KV_LORA), F32)],
        compiler_params=_params(("parallel", "parallel", "arbitrary"), vmem + 4 * 2**20),
        name="mla_prompt_attn",
    )(q_full, k_full)


def _ckv_page_copy(ckv_hbm, ckv_buf, sem, layer, page, slot, j, page_size):
    return pltpu.make_async_copy(ckv_hbm.at[layer, page], ckv_buf.at[slot, pl.ds(j * page_size, page_size)],
                                 sem.at[slot])


def _krt_page_copy(krt_hbm, krt_buf, sem, layer, page, slot, j, page_size):
    return pltpu.make_async_copy(krt_hbm.at[layer, page], krt_buf.at[slot, :, pl.ds(j * page_size, page_size)],
                                 sem.at[slot])


def _mla_sample_body(pt_ref, q_ref, kn_ref, ckv_hbm, krt_hbm, o_ref, ckv_buf, krt_buf, kbf_ref, p_ref, onew_ref,
                     sem_c, sem_r, *, layer, n_pages, page_size, chunk, t_new):
    n = pl.program_id(0)
    n_samples = pl.num_programs(0) - 1
    past = n_pages * page_size

    def start_fetch(sample, slot):
        for j in range(n_pages):
            page = pt_ref[sample, j]
            _ckv_page_copy(ckv_hbm, ckv_buf, sem_c, layer, page, slot, j, page_size).start()
            _krt_page_copy(krt_hbm, krt_buf, sem_r, layer, page, slot, j, page_size).start()

    def wait_fetch(slot):
        for j in range(n_pages):
            _ckv_page_copy(ckv_hbm, ckv_buf, sem_c, layer, 0, slot, j, page_size).wait()
            _krt_page_copy(krt_hbm, krt_buf, sem_r, layer, 0, slot, j, page_size).wait()

    slot = lax.rem(n, 2)

    @pl.when(n == 0)
    def _():
        start_fetch(0, 0)
        kbf_ref[1] = jnp.zeros(kbf_ref.shape[1:], BF16)
        p_ref[...] = jnp.zeros(p_ref.shape, BF16)
        onew_ref[...] = jnp.zeros(onew_ref.shape, F32)

    @pl.when(n + 1 < n_samples)
    def _():
        start_fetch(n + 1, 1 - slot)

    @pl.when(n < n_samples)
    def _():
        wait_fetch(slot)

    def step(cur):
        prev = 1 - cur
        o_ref[...] = (_dot(p_ref[...], kbf_ref[prev]) + onew_ref[...]).astype(BF16)

        q = q_ref[...]
        q_lat = q[:, :KV_LORA]
        q_rope = q[:, KV_LORA:KV_LORA + MLA_ROPE]
        rows = q.shape[0]
        for c in range(past // chunk):
            sl = pl.ds(c * chunk, chunk)
            kbf_ref[cur, sl, :] = ckv_buf[cur, sl, :].astype(BF16)
        s_past = _dot_nt(q_lat, kbf_ref[cur]) + _dot(q_rope, krt_buf[cur].astype(BF16))

        kn = kn_ref[...]
        s_new = _dot_nt(q, kn)
        q_tok = lax.broadcasted_iota(jnp.int32, (rows, t_new), 0) // MLA_HEADS
        k_tok = lax.broadcasted_iota(jnp.int32, (rows, t_new), 1)
        s_new = jnp.where(k_tok <= q_tok, s_new, NEG)

        m = jnp.maximum(jnp.max(s_past, axis=-1, keepdims=True), jnp.max(s_new, axis=-1, keepdims=True))
        p_past = jnp.exp(s_past - m)
        p_new = jnp.exp(s_new - m)
        denom = jnp.sum(p_past, axis=-1, keepdims=True) + jnp.sum(p_new, axis=-1, keepdims=True)
        p_ref[...] = (p_past / denom).astype(BF16)
        onew_ref[...] = _dot((p_new / denom).astype(BF16), kn[:, :KV_LORA])

    @pl.when(slot == 0)
    def _():
        step(0)

    @pl.when(slot == 1)
    def _():
        step(1)


def _mla_sample(page_table, q3, kn3, cache_ckv, cache_krope_t, layer, *, chunk):
    n_samples, rows, _ = q3.shape
    t_new = kn3.shape[1]
    n_pages = page_table.shape[1]
    page_size = cache_ckv.shape[2]
    past = n_pages * page_size
    vmem = 2 * past * (KV_LORA + MLA_ROPE) * 4 + past * (2 * KV_LORA + MLA_ROPE) * 2 + rows * past * 4 * 5 \
        + 2 * chunk * KV_LORA * 4
    cur = lambda n, pt: (jnp.minimum(n, n_samples - 1), 0, 0)
    grid_spec = pltpu.PrefetchScalarGridSpec(
        num_scalar_prefetch=1,
        grid=(n_samples + 1,),
        in_specs=[
            pl.BlockSpec((None, rows, QK_WIDTH), cur),
            pl.BlockSpec((None, t_new, QK_WIDTH), cur),
            pl.BlockSpec(memory_space=pl.ANY),
            pl.BlockSpec(memory_space=pl.ANY),
        ],
        out_specs=pl.BlockSpec((None, rows, KV_LORA), lambda n, pt: (jnp.maximum(n - 1, 0), 0, 0)),
        scratch_shapes=[
            pltpu.VMEM((2, past, KV_LORA), F32),
            pltpu.VMEM((2, MLA_ROPE, past), F32),
            pltpu.VMEM((2, past, KV_LORA), BF16),
            pltpu.VMEM((rows, past), BF16),
            pltpu.VMEM((rows, KV_LORA), F32),
            pltpu.SemaphoreType.DMA((2,)),
            pltpu.SemaphoreType.DMA((2,)),
        ],
    )
    return pl.pallas_call(
        functools.partial(_mla_sample_body, layer=layer, n_pages=n_pages, page_size=page_size, chunk=chunk,
                          t_new=t_new),
        out_shape=jax.ShapeDtypeStruct((n_samples, rows, KV_LORA), BF16),
        grid_spec=grid_spec,
        compiler_params=_params(("arbitrary",), vmem + 4 * 2**20),
        name="mla_sample_attn",
    )(page_table, q3, kn3, cache_ckv, cache_krope_t)


def _sg_body(h_ref, w_ref, b_ref, g_ref, beta_ref, y_ref, *v_refs, period, tm):
    row = lax.broadcasted_iota(jnp.int32, (CHUNK, CHUNK), 0)
    col = lax.broadcasted_iota(jnp.int32, (CHUNK, CHUNK), 1)
    shift = period.bit_length() - 1
    keep = (col <= row) & ((row >> shift) == (col >> shift))
    w = [jnp.where(keep, w_ref[h], 0.0).astype(BF16) for h in range(SG_HEADS)]
    bias = b_ref[...]
    for c in range(tm // CHUNK):
        rows = pl.ds(c * CHUNK, CHUNK)
        a = jax.nn.gelu(h_ref[rows, :])
        u = a[:, :GROUP_WIDTH]
        v = _ln(a[:, GROUP_WIDTH:], g_ref[...], beta_ref[...])
        if v_refs:
            v_refs[0][rows, :] = v
        vb = v.astype(BF16)
        for h in range(SG_HEADS):
            sl = slice(h * SG_HEAD_DIM, (h + 1) * SG_HEAD_DIM)
            g = _dot(w[h], vb[:, sl]) + bias[:, sl]
            y_ref[rows, sl] = (u[:, sl] * g).astype(BF16)


def _spatial_gate(h, w, b_rows, ln_g, ln_b, layer, *, period, with_v, tm):
    m = h.shape[0]
    const = lambda *shape: pl.BlockSpec((None,) + shape, lambda i: (layer,) + (0,) * len(shape))
    out_shape = [jax.ShapeDtypeStruct((m, GROUP_WIDTH), BF16)]
    out_specs = [pl.BlockSpec((tm, GROUP_WIDTH), lambda i: (i, 0))]
    if with_v:
        out_shape.append(jax.ShapeDtypeStruct((m, GROUP_WIDTH), F32))
        out_specs.append(pl.BlockSpec((tm, GROUP_WIDTH), lambda i: (i, 0)))
    vmem = 2 * tm * 2 * GROUP_WIDTH * 4 + 2 * tm * GROUP_WIDTH * 6 + 16 * CHUNK * 2 * GROUP_WIDTH * 4
    return pl.pallas_call(
        functools.partial(_sg_body, period=period, tm=tm),
        out_shape=tuple(out_shape),
        grid=(m // tm,),
        in_specs=[
            pl.BlockSpec((tm, 2 * GROUP_WIDTH), lambda i: (i, COL_B // (2 * GROUP_WIDTH))),
            const(SG_HEADS, CHUNK, CHUNK), const(CHUNK, GROUP_WIDTH), const(1, GROUP_WIDTH), const(1, GROUP_WIDTH),
        ],
        out_specs=tuple(out_specs),
        compiler_params=_params(("parallel",), vmem + 4 * 2**20),
        name="spatial_gate",
    )(h, w, b_rows, ln_g, ln_b)


def _conv_tail(y, ln_g, ln_b):
    return jax.nn.silu(_ln(y, ln_g, ln_b))


def _pc_prompt_body(hc_ref, hd_ref, pw_ref, ps_ref, cw_ref, cb_ref, g_ref, beta_ref,
                    yc_ref, yd_ref, pool_out_ref, conv_out_ref, cext, zext, zsh, ybuf, *, tl, rb):
    li = pl.program_id(1)

    @pl.when(li == 0)
    def _():
        cext[0:HALO_POOL, :] = jnp.zeros((HALO_POOL, GROUP_WIDTH), F32)
        zext[0:HALO_CONV, :] = jnp.zeros((HALO_CONV, GROUP_WIDTH), F32)

    hc = hc_ref[...]
    cext[HALO_POOL:HALO_POOL + tl, :] = hc
    pos = li * tl + lax.broadcasted_iota(jnp.int32, (tl, 1), 0)
    for gi, win in enumerate(POOL_WINDOWS):
        sl = slice(gi * POOL_GROUP, (gi + 1) * POOL_GROUP)
        acc = hc[:, sl]
        for k in range(1, win):
            acc = acc + cext[pl.ds(HALO_POOL - k, tl), sl]
        cnt = jnp.minimum(pos + 1, win).astype(F32)
        pooled = acc / cnt - hc[:, sl]
        y = _dot(pooled.astype(BF16), pw_ref[gi]) * ps_ref[:, sl]
        yc_ref[:, sl] = y.astype(BF16)

    hd = hd_ref[...]
    zext[HALO_CONV:HALO_CONV + tl, :] = hd[:, :GROUP_WIDTH] * jax.nn.sigmoid(hd[:, GROUP_WIDTH:])
    first = HALO_CONV - CONV_STATE
    for b in range(SUBLANES):
        span = tl + (CONV_WIDTH - 1 - b) // SUBLANES * SUBLANES
        zsh[b, 0:span, :] = zext[pl.ds(first + b, span), :]
    for r0 in range(0, tl, rb):
        for cg in range(GROUP_WIDTH // LANES):
            sl = slice(cg * LANES, (cg + 1) * LANES)
            acc = jnp.broadcast_to(cb_ref[:, sl], (rb, LANES))
            for k in range(CONV_WIDTH):
                acc = acc + cw_ref[k:k + 1, sl] * zsh[k % SUBLANES, pl.ds(r0 + k // SUBLANES * SUBLANES, rb), sl]
            ybuf[r0:r0 + rb, sl] = acc
    yd_ref[...] = _conv_tail(ybuf[...], g_ref[...], beta_ref[...]).astype(BF16)

    @pl.when(li == pl.num_programs(1) - 1)
    def _():
        pool_out_ref[...] = cext[pl.ds(HALO_POOL + tl - POOL_STATE, POOL_STATE), :]
        conv_out_ref[...] = zext[pl.ds(HALO_CONV + tl - CONV_STATE, CONV_STATE), :]

    cext[0:HALO_POOL, :] = cext[tl:tl + HALO_POOL, :]
    zext[0:HALO_CONV, :] = zext[tl:tl + HALO_CONV, :]


def _pool_conv_prompt(h, n_batch, pool_w, pool_scale, conv_w, conv_b, ln_g, ln_b, layer, *, tl, rb):
    m = h.shape[0]
    nl = m // n_batch // tl
    const = lambda *shape: pl.BlockSpec((None,) + shape, lambda b, l: (layer,) + (0,) * len(shape))
    y = jax.ShapeDtypeStruct((m, GROUP_WIDTH), BF16)
    vmem = 2 * tl * 3 * GROUP_WIDTH * 4 + 2 * 2 * tl * GROUP_WIDTH * 2 + 3 * (tl + HALO_CONV) * GROUP_WIDTH * 4 \
        + 8 * tl * GROUP_WIDTH * 4
    return pl.pallas_call(
        functools.partial(_pc_prompt_body, tl=tl, rb=rb),
        out_shape=(y, y, jax.ShapeDtypeStruct((n_batch, POOL_STATE, GROUP_WIDTH), F32),
                   jax.ShapeDtypeStruct((n_batch, CONV_STATE, GROUP_WIDTH), F32)),
        grid=(n_batch, nl),
        in_specs=[
            pl.BlockSpec((tl, GROUP_WIDTH), lambda b, l: (b * nl + l, COL_C // GROUP_WIDTH)),
            pl.BlockSpec((tl, 2 * GROUP_WIDTH), lambda b, l: (b * nl + l, COL_D // (2 * GROUP_WIDTH))),
            const(len(POOL_WINDOWS), POOL_GROUP, POOL_GROUP), const(1, GROUP_WIDTH),
            const(CONV_WIDTH, GROUP_WIDTH), const(1, GROUP_WIDTH), const(1, GROUP_WIDTH), const(1, GROUP_WIDTH),
        ],
        out_specs=(pl.BlockSpec((tl, GROUP_WIDTH), lambda b, l: (b * nl + l, 0)),
                   pl.BlockSpec((tl, GROUP_WIDTH), lambda b, l: (b * nl + l, 0)),
                   pl.BlockSpec((None, POOL_STATE, GROUP_WIDTH), lambda b, l: (b, 0, 0)),
                   pl.BlockSpec((None, CONV_STATE, GROUP_WIDTH), lambda b, l: (b, 0, 0))),
        scratch_shapes=[pltpu.VMEM((HALO_POOL + tl, GROUP_WIDTH), F32), pltpu.VMEM((HALO_CONV + tl, GROUP_WIDTH), F32),
                        pltpu.VMEM((SUBLANES, tl + HALO_CONV - SUBLANES, GROUP_WIDTH), F32),
                        pltpu.VMEM((tl, GROUP_WIDTH), F32)],
        compiler_params=_params(("parallel", "arbitrary"), vmem + SUBLANES * (tl + HALO_CONV) * GROUP_WIDTH * 4 + 4 * 2**20),
        name="pool_conv_prompt",
    )(h, h, pool_w, pool_scale, conv_w, conv_b, ln_g, ln_b)


def _pc_sample_body(hc_ref, hd_ref, sp_ref, sc_ref, pw_ref, ps_ref, cw_ref, cb_ref, g_ref, beta_ref,
                    yc_ref, yd_ref, pool_out_ref, conv_out_ref, cext, zext, ybuf, *, bs, t_new, past_len):
    rows = bs * t_new
    hc = hc_ref[...]
    cext[:, 0:POOL_STATE, :] = sp_ref[...]
    cext[:, POOL_STATE:POOL_STATE + t_new, :] = hc.reshape(bs, t_new, GROUP_WIDTH)
    pos = past_len + lax.broadcasted_iota(jnp.int32, (1, t_new, 1), 1)
    for gi, win in enumerate(POOL_WINDOWS):
        sl = slice(gi * POOL_GROUP, (gi + 1) * POOL_GROUP)
        acc = cext[:, pl.ds(POOL_STATE, t_new), sl]
        for k in range(1, win):
            acc = acc + cext[:, pl.ds(POOL_STATE - k, t_new), sl]
        cnt = jnp.minimum(pos + 1, win).astype(F32)
        pooled = (acc / cnt).reshape(rows, POOL_GROUP) - hc[:, sl]
        y = _dot(pooled.astype(BF16), pw_ref[gi]) * ps_ref[:, sl]
        yc_ref[:, sl] = y.astype(BF16)
    pool_out_ref[...] = cext[:, pl.ds(t_new, POOL_STATE), :]

    hd = hd_ref[...]
    z = hd[:, :GROUP_WIDTH] * jax.nn.sigmoid(hd[:, GROUP_WIDTH:])
    zext[:, 0:CONV_STATE, :] = sc_ref[...]
    zext[:, CONV_STATE:CONV_STATE + t_new, :] = z.reshape(bs, t_new, GROUP_WIDTH)
    for cg in range(GROUP_WIDTH // LANES):
        sl = slice(cg * LANES, (cg + 1) * LANES)
        acc = jnp.broadcast_to(cb_ref[:, sl].reshape(1, 1, LANES), (bs, t_new, LANES))
        for k in range(CONV_WIDTH):
            acc = acc + cw_ref[k:k + 1, sl].reshape(1, 1, LANES) * zext[:, pl.ds(k, t_new), sl]
        ybuf[:, sl] = acc.reshape(rows, LANES)
    yd_ref[...] = _conv_tail(ybuf[...], g_ref[...], beta_ref[...]).astype(BF16)
    conv_out_ref[...] = zext[:, pl.ds(t_new, CONV_STATE), :]


def _pool_conv_sample(h, state_pool, state_conv, pool_w, pool_scale, conv_w, conv_b, ln_g, ln_b, layer, *,
                      bs, t_new, past_len):
    m = h.shape[0]
    n_samples = m // t_new
    rows = bs * t_new
    const = lambda *shape: pl.BlockSpec((None,) + shape, lambda i: (layer,) + (0,) * len(shape))
    y = jax.ShapeDtypeStruct((m, GROUP_WIDTH), BF16)
    pool_rows = POOL_STATE + t_new + 1
    conv_rows = CONV_STATE + t_new + 2
    vmem = 2 * rows * 3 * GROUP_WIDTH * 4 + 4 * bs * (16 + 32) * GROUP_WIDTH * 4 \
        + bs * (pool_rows + conv_rows) * GROUP_WIDTH * 4 + 8 * rows * GROUP_WIDTH * 4
    return pl.pallas_call(
        functools.partial(_pc_sample_body, bs=bs, t_new=t_new, past_len=past_len),
        out_shape=(y, y, jax.ShapeDtypeStruct((n_samples, POOL_STATE, GROUP_WIDTH), F32),
                   jax.ShapeDtypeStruct((n_samples, CONV_STATE, GROUP_WIDTH), F32)),
        grid=(n_samples // bs,),
        in_specs=[
            pl.BlockSpec((rows, GROUP_WIDTH), lambda i: (i, COL_C // GROUP_WIDTH)),
            pl.BlockSpec((rows, 2 * GROUP_WIDTH), lambda i: (i, COL_D // (2 * GROUP_WIDTH))),
            pl.BlockSpec((None, bs, POOL_STATE, GROUP_WIDTH), lambda i: (layer, i, 0, 0)),
            pl.BlockSpec((None, bs, CONV_STATE, GROUP_WIDTH), lambda i: (layer, i, 0, 0)),
            const(len(POOL_WINDOWS), POOL_GROUP, POOL_GROUP), const(1, GROUP_WIDTH),
            const(CONV_WIDTH, GROUP_WIDTH), const(1, GROUP_WIDTH), const(1, GROUP_WIDTH), const(1, GROUP_WIDTH),
        ],
        out_specs=(pl.BlockSpec((rows, GROUP_WIDTH), lambda i: (i, 0)),
                   pl.BlockSpec((rows, GROUP_WIDTH), lambda i: (i, 0)),
                   pl.BlockSpec((bs, POOL_STATE, GROUP_WIDTH), lambda i: (i, 0, 0)),
                   pl.BlockSpec((bs, CONV_STATE, GROUP_WIDTH), lambda i: (i, 0, 0))),
        scratch_shapes=[pltpu.VMEM((bs, pool_rows, GROUP_WIDTH), F32), pltpu.VMEM((bs, conv_rows, GROUP_WIDTH), F32),
                        pltpu.VMEM((rows, GROUP_WIDTH), F32)],
        compiler_params=_params(("parallel",), vmem + 4 * 2**20),
        name="pool_conv_sample",
    )(h, h, state_pool, state_conv, pool_w, pool_scale, conv_w, conv_b, ln_g, ln_b)


def _mix_out_body(x_ref, ol_ref, yb_ref, yc_ref, yd_ref, wuv_ref, wo_ref, o_ref):
    ol = ol_ref[...]
    parts = [_dot(ol[:, h * KV_LORA:(h + 1) * KV_LORA], wuv_ref[h]).astype(BF16) for h in range(MLA_HEADS)]
    y = jnp.concatenate(parts + [yb_ref[...], yc_ref[...], yd_ref[...]], axis=1)
    o_ref[...] = x_ref[...] + _dot(y, wo_ref[...])


def _mix_out(x, o_lat, y_b, y_c, y_d, w_uv, w_out, layer, *, tm):
    m, d = x.shape
    mw = w_out.shape[1]
    row = lambda w: pl.BlockSpec((tm, w), lambda i: (i, 0))
    vmem = 2 * 2 * tm * d * 4 + 2 * tm * (MLA_HEADS * KV_LORA + 3 * GROUP_WIDTH) * 2 + 2 * mw * d * 2 \
        + tm * mw * 2 + tm * d * 4
    return pl.pallas_call(
        _mix_out_body,
        out_shape=jax.ShapeDtypeStruct((m, d), F32),
        grid=(m // tm,),
        in_specs=[
            row(d), row(MLA_HEADS * KV_LORA), row(GROUP_WIDTH), row(GROUP_WIDTH), row(GROUP_WIDTH),
            pl.BlockSpec((None, MLA_HEADS, KV_LORA, 128), lambda i: (layer, 0, 0, 0)),
            pl.BlockSpec((None, mw, d), lambda i: (layer, 0, 0)),
        ],
        out_specs=row(d),
        compiler_params=_params(("parallel",), vmem + 4 * 2**20),
        name="mix_out_proj",
    )(x, o_lat, y_b, y_c, y_d, w_uv, w_out)


def _cross_prompt_body(x_ref, g_ref, wq_ref, wo_ref, mk_ref, mv_ref, o_ref):
    x = x_ref[...]
    xn = _rms(x, g_ref[...]).astype(BF16)
    q = (_dot(xn, wq_ref[...]) * MEM_HEAD_DIM ** -0.5).astype(BF16)
    k = mk_ref[...].astype(BF16)
    v = mv_ref[...].astype(BF16)
    heads = []
    for h in range(MEM_HEADS):
        sl = slice(h * MEM_HEAD_DIM, (h + 1) * MEM_HEAD_DIM)
        p = _softmax(_dot_nt(q[:, sl], k[:, sl]))
        heads.append(_dot(p.astype(BF16), v[:, sl]).astype(BF16))
    o_ref[...] = x + _dot(jnp.concatenate(heads, axis=1), wo_ref[...])


def _cross_prompt(x, gain, w_cq, w_co, mem_k, mem_v, layer, *, tq):
    m, d = x.shape
    n_batch, mem_len = mem_k.shape[1:3]
    nq = m // n_batch // tq
    const = lambda *shape: pl.BlockSpec((None,) + shape, lambda b, i: (layer,) + (0,) * len(shape))
    vmem = 2 * 2 * tq * d * 4 + 2 * 2 * d * MEM_WIDTH * 2 + 2 * 2 * mem_len * MEM_WIDTH * 4 \
        + tq * d * 6 + 8 * tq * MEM_WIDTH * 4
    return pl.pallas_call(
        _cross_prompt_body,
        out_shape=jax.ShapeDtypeStruct((m, d), F32),
        grid=(n_batch, nq),
        in_specs=[
            pl.BlockSpec((tq, d), lambda b, i: (b * nq + i, 0)),
            const(1, d), const(d, MEM_WIDTH), const(MEM_WIDTH, d),
            pl.BlockSpec((None, None, mem_len, MEM_WIDTH), lambda b, i: (layer, b, 0, 0)),
            pl.BlockSpec((None, None, mem_len, MEM_WIDTH), lambda b, i: (layer, b, 0, 0)),
        ],
        out_specs=pl.BlockSpec((tq, d), lambda b, i: (b * nq + i, 0)),
        compiler_params=_params(("parallel", "parallel"), vmem + 4 * 2**20),
        name="cross_attn_prompt",
    )(x, gain, w_cq, w_co, mem_k, mem_v)


def _cross_sample_body(x_ref, g_ref, wq_ref, wo_ref, mk_ref, mv_ref, o_ref, attn_ref, *, bs, t_new):
    x = x_ref[...]
    xn = _rms(x, g_ref[...]).astype(BF16)
    q = _dot(xn, wq_ref[...]) * MEM_HEAD_DIM ** -0.5
    rows = MEM_HEADS * t_new
    lane_head = lax.broadcasted_iota(jnp.int32, (rows, MEM_WIDTH), 1) // MEM_HEAD_DIM
    row_head = lax.broadcasted_iota(jnp.int32, (rows, MEM_WIDTH), 0) // t_new
    own = lane_head == row_head
    for s in range(bs):
        qs = q[s * t_new:(s + 1) * t_new, :]
        q_bd = jnp.where(own, jnp.concatenate([qs] * MEM_HEADS, axis=0), 0.0).astype(BF16)
        p = _softmax(_dot_nt(q_bd, mk_ref[s].astype(BF16)))
        r = jnp.where(own, _dot(p.astype(BF16), mv_ref[s].astype(BF16)), 0.0)
        o = r[0:t_new]
        for h in range(1, MEM_HEADS):
            o = o + r[h * t_new:(h + 1) * t_new]
        attn_ref[s * t_new:(s + 1) * t_new, :] = o
    o_ref[...] = x + _dot(attn_ref[...].astype(BF16), wo_ref[...])


def _cross_sample(x, gain, w_cq, w_co, mem_k, mem_v, layer, *, bs, t_new):
    m, d = x.shape
    mem_len = mem_k.shape[2]
    rows = bs * t_new
    const = lambda *shape: pl.BlockSpec((None,) + shape, lambda i: (layer,) + (0,) * len(shape))
    vmem = 2 * 2 * rows * d * 4 + 2 * 2 * d * MEM_WIDTH * 2 + 2 * 2 * bs * mem_len * MEM_WIDTH * 4 \
        + rows * d * 6 + 16 * mem_len * MEM_WIDTH * 4
    return pl.pallas_call(
        functools.partial(_cross_sample_body, bs=bs, t_new=t_new),
        out_shape=jax.ShapeDtypeStruct((m, d), F32),
        grid=(m // rows,),
        in_specs=[
            pl.BlockSpec((rows, d), lambda i: (i, 0)),
            const(1, d), const(d, MEM_WIDTH), const(MEM_WIDTH, d),
            pl.BlockSpec((None, bs, mem_len, MEM_WIDTH), lambda i: (layer, i, 0, 0)),
            pl.BlockSpec((None, bs, mem_len, MEM_WIDTH), lambda i: (layer, i, 0, 0)),
        ],
        out_specs=pl.BlockSpec((rows, d), lambda i: (i, 0)),
        scratch_shapes=[pltpu.VMEM((rows, MEM_WIDTH), F32)],
        compiler_params=_params(("parallel",), vmem + 4 * 2**20),
        name="cross_attn_sample",
    )(x, gain, w_cq, w_co, mem_k, mem_v)


def _rotate_half_cols(w):
    half = w.shape[-1] // 2
    return jnp.concatenate([-w[..., half:], w[..., :half]], axis=-1)


def _pad_cols(w, width):
    return jnp.pad(w, [(0, 0)] * (w.ndim - 1) + [(0, width - w.shape[-1])])


def _rope_tables(pos):
    half = MLA_ROPE // 2
    inv = ROPE_THETA ** (-jnp.arange(half, dtype=F32) / half)
    ang = pos.astype(F32)[:, None] * inv[None, :]
    c = jnp.cos(ang)
    s = jnp.sin(ang)
    return (_pad_cols(jnp.concatenate([c, c], axis=1), ROPE_PAD), _pad_cols(jnp.concatenate([s, s], axis=1), ROPE_PAD))


def _pick_tile(m, pref):
    t = min(m, pref)
    while m % t:
        t //= 2
    return t


def kernel(x_prompt, x_sample, mem_prompt, cache_ckv, cache_krope, cache_mem_k, cache_mem_v, state_pool, state_conv, page_table, w_in, w_out, norm_q_lat, w_qb, norm_kv_lat, w_uk, w_uv, sg_w, sg_b, sg_ln_g, sg_ln_b, pool_w, pool_scale, conv_w, conv_b, conv_ln_g, conv_ln_b, norm_mix, norm_ffn_a, w_ffn_a_in, w_ffn_a_out, norm_ffn_b, w_ffn_b_in, w_ffn_b_out, norm_cross, norm_mem, w_cq, w_ck, w_cv, w_co, norm_final):
    n_b, seq, d = x_prompt.shape
    n_db, t_new, _ = x_sample.shape
    depth = w_in.shape[0]
    n_pages = page_table.shape[1]
    page_size = cache_ckv.shape[2]
    past_len = n_pages * page_size
    mem_len = mem_prompt.shape[1]

    a_cols = Q_LORA + KV_LORA + MLA_ROPE
    b0, c0, d0 = a_cols, a_cols + 2 * GROUP_WIDTH, a_cols + 3 * GROUP_WIDTH
    w_kr = w_in[:, :, Q_LORA + KV_LORA:a_cols]
    w_in_r = jnp.concatenate([
        w_in[:, :, b0:c0], w_in[:, :, d0:], w_in[:, :, :Q_LORA], w_in[:, :, c0:d0],
        w_in[:, :, Q_LORA:Q_LORA + KV_LORA], _pad_cols(w_kr, ROPE_PAD), _pad_cols(_rotate_half_cols(w_kr), ROPE_PAD),
    ], axis=-1).astype(BF16)
    w_q_nope = w_qb[..., :MLA_NOPE].reshape(depth, Q_LORA, MLA_HEADS * MLA_NOPE).astype(BF16)
    w_q_r = w_qb[..., MLA_NOPE:]
    w_q_rope = _pad_cols(w_q_r, ROPE_PAD).reshape(depth, Q_LORA, MLA_HEADS * ROPE_PAD).astype(BF16)
    w_q_ropep = _pad_cols(_rotate_half_cols(w_q_r), ROPE_PAD).reshape(depth, Q_LORA, MLA_HEADS * ROPE_PAD).astype(BF16)
    w_ukt = jnp.transpose(w_uk, (0, 2, 3, 1)).astype(BF16)
    w_uvh = jnp.transpose(w_uv, (0, 2, 1, 3)).astype(BF16)
    w_out_b = w_out.astype(BF16)
    w_fa_in, w_fa_out = w_ffn_a_in.astype(BF16), w_ffn_a_out.astype(BF16)
    w_fb_in, w_fb_out = w_ffn_b_in.astype(BF16), w_ffn_b_out.astype(BF16)
    w_cq_b, w_ck_b, w_cv_b, w_co_b = (w.astype(BF16) for w in (w_cq, w_ck, w_cv, w_co))
    pool_w_b = pool_w.astype(BF16)
    row3 = lambda g: g.reshape(depth, 1, g.shape[-1])
    n_mix, n_fa, n_fb, n_cr, n_mem = (row3(g) for g in (norm_mix, norm_ffn_a, norm_ffn_b, norm_cross, norm_mem))
    n_q, n_kv = row3(norm_q_lat), row3(norm_kv_lat)
    sg_g, sg_beta = row3(sg_ln_g), row3(sg_ln_b)
    p_scale, c_b, c_g, c_beta = row3(pool_scale), row3(conv_b), row3(conv_ln_g), row3(conv_ln_b)
    n_fin = norm_final.reshape(1, d)
    reps = CHUNK // t_new
    sg_w_p = sg_w
    sg_w_s = jnp.tile(sg_w[:, :, :t_new, :t_new], (1, 1, reps, reps))
    bias_rows = lambda b: jnp.repeat(jnp.transpose(b, (0, 2, 1)), SG_HEAD_DIM, axis=2)
    sg_b_p = bias_rows(sg_b)
    sg_b_s = jnp.tile(bias_rows(sg_b[:, :, :t_new]), (1, reps, 1))

    cache_krope_t = jnp.swapaxes(cache_krope, 2, 3)

    cos_p, sin_p = _rope_tables(jnp.tile(jnp.arange(seq), n_b))
    cos_s, sin_s = _rope_tables(jnp.tile(past_len + jnp.arange(t_new), n_db))

    mp = n_b * seq
    ms = n_db * t_new
    xp = x_prompt.reshape(mp, d)
    xs = x_sample.reshape(ms, d)

    mem_k, mem_v = _mem_kv(mem_prompt.reshape(n_b * mem_len, d), n_mem, w_ck_b, w_cv_b, tm=_pick_tile(n_b * mem_len, 512))
    mem_k = mem_k.reshape(depth, n_b, mem_len, MEM_WIDTH)
    mem_v = mem_v.reshape(depth, n_b, mem_len, MEM_WIDTH)

    tm_p = _pick_tile(mp, 512)
    tm_s = _pick_tile(ms, 512)
    tmf_p = _pick_tile(mp, 1024)
    tmf_s = _pick_tile(ms, 1024)
    tf = 512
    bs_pc = _pick_tile(n_db, 16)
    bs_cr = _pick_tile(n_db, 8)
    ckv_p, kr_p, pool_p, conv_p = [], [], [], []
    ckv_s, kr_s, sgv_s, pool_s, conv_s = [], [], [], [], []
    for l in range(depth):
        last = l == depth - 1
        xp = _ffn(xp, n_fa, w_fa_in, w_fa_out, n_fin, l, final_norm=False, tm=tmf_p, tf=tf)
        h = _rms_proj(xp, n_mix, w_in_r, l, tm=_pick_tile(mp, 1024), tn=1792)
        q_full, k_full, ckv, kr = _mla_proj(h, cos_p, sin_p, n_q, n_kv, w_q_nope, w_q_rope, w_q_ropep, w_ukt, l, tm=tm_p)
        o_lat = _mla_prompt(q_full, k_full, n_b, tq=_pick_tile(seq, 512))
        (y_b,) = _spatial_gate(h, sg_w_p, sg_b_p, sg_g, sg_beta, l, period=CHUNK, with_v=False, tm=tm_p)
        y_c, y_d, pst, cst = _pool_conv_prompt(h, n_b, pool_w_b, p_scale, conv_w, c_b, c_g, c_beta, l,
                                               tl=_pick_tile(seq, 256), rb=128)
        xp = _mix_out(xp, o_lat, y_b, y_c, y_d, w_uvh, w_out_b, l, tm=tm_p)
        xp = _cross_prompt(xp, n_cr, w_cq_b, w_co_b, mem_k, mem_v, l, tq=_pick_tile(seq, 512))
        xp = _ffn(xp, n_fb, w_fb_in, w_fb_out, n_fin, l, final_norm=last, tm=tmf_p, tf=tf)
        ckv_p.append(ckv); kr_p.append(kr); pool_p.append(pst); conv_p.append(cst)
        xs = _ffn(xs, n_fa, w_fa_in, w_fa_out, n_fin, l, final_norm=False, tm=tmf_s, tf=tf)
        h = _rms_proj(xs, n_mix, w_in_r, l, tm=_pick_tile(ms, 1024), tn=1792)
        q_full, k_full, ckv, kr = _mla_proj(h, cos_s, sin_s, n_q, n_kv, w_q_nope, w_q_rope, w_q_ropep, w_ukt, l, tm=tm_s)
        o_lat = _mla_sample(page_table, q_full.reshape(n_db, t_new * MLA_HEADS, QK_WIDTH),
                            k_full.reshape(n_db, t_new, QK_WIDTH), cache_ckv, cache_krope_t, l,
                            chunk=_pick_tile(past_len, 1024))
        o_lat = o_lat.reshape(ms, MLA_HEADS * KV_LORA)
        y_b, v_b = _spatial_gate(h, sg_w_s, sg_b_s, sg_g, sg_beta, l, period=t_new, with_v=True, tm=tm_s)
        y_c, y_d, pst, cst = _pool_conv_sample(h, state_pool, state_conv, pool_w_b, p_scale, conv_w, c_b, c_g, c_beta,
                                               l, bs=bs_pc, t_new=t_new, past_len=past_len)
        xs = _mix_out(xs, o_lat, y_b, y_c, y_d, w_uvh, w_out_b, l, tm=tm_s)
        xs = _cross_sample(xs, n_cr, w_cq_b, w_co_b, cache_mem_k, cache_mem_v, l, bs=bs_cr, t_new=t_new)
        xs = _ffn(xs, n_fb, w_fb_in, w_fb_out, n_fin, l, final_norm=last, tm=tmf_s, tf=tf)
        ckv_s.append(ckv); kr_s.append(kr); sgv_s.append(v_b); pool_s.append(pst); conv_s.append(cst)

    stack_p = lambda xs_, w: jnp.stack(xs_).reshape(depth, n_b, seq, w)
    stack_s = lambda xs_, w: jnp.stack(xs_).reshape(depth, n_db, t_new, w)
    return (xp.reshape(n_b, seq, d), xs.reshape(n_db, t_new, d),
            stack_p(ckv_p, KV_LORA), stack_p(kr_p, MLA_ROPE), mem_k, mem_v, jnp.stack(pool_p), jnp.stack(conv_p),
            stack_s(ckv_s, KV_LORA), stack_s(kr_s, MLA_ROPE), stack_s(sgv_s, GROUP_WIDTH),
            jnp.stack(pool_s), jnp.stack(conv_s))
```

```python
import functools

import jax
import jax.numpy as jnp
from jax import lax
from jax.experimental import pallas as pl
from jax.experimental.pallas import tpu as pltpu

F32 = jnp.float32
BF16 = jnp.bfloat16

GROUP_WIDTH = 512
MLA_HEADS = 4
MLA_NOPE = 128
MLA_ROPE = 64
KV_LORA = 256
Q_LORA = 512
MLA_SCALE = (MLA_NOPE + MLA_ROPE) ** -0.5
ROPE_THETA = 10000.0
SG_HEADS = 4
SG_HEAD_DIM = 128
CHUNK = 128
POOL_WINDOWS = (2, 4, 8, 16)
POOL_GROUP = 128
POOL_STATE = 15
CONV_WIDTH = 31
CONV_STATE = 30
MEM_HEADS = 4
MEM_HEAD_DIM = 128
MEM_WIDTH = 512
EPS = 1e-6
NEG = -1e30

LANES = 128
SUBLANES = 8
V7X_VMEM_BYTES = 64 * 2**20

ROPE_PAD = LANES
COL_B = 0
COL_D = 1024
COL_QC = 2048
COL_C = 2560
COL_CKV = 3072
COL_KR = 3328
COL_KRP = 3456
H_COLS = 3584
QK_WIDTH = KV_LORA + ROPE_PAD
HALO_POOL = 16
HALO_CONV = 32


def _params(semantics, vmem_bytes):
    return pltpu.CompilerParams(dimension_semantics=semantics,
                                vmem_limit_bytes=int(min(vmem_bytes, V7X_VMEM_BYTES - 8 * 2**20)))


def _rms(x, g):
    return x * lax.rsqrt(jnp.mean(x * x, axis=-1, keepdims=True) + EPS) * g


def _ln(x, g, b):
    mu = jnp.mean(x, axis=-1, keepdims=True)
    xc = x - mu
    return xc * lax.rsqrt(jnp.mean(xc * xc, axis=-1, keepdims=True) + EPS) * g + b


def _dot(a, b):
    return jnp.dot(a, b, preferred_element_type=F32)


def _dot_nt(a, b):
    return lax.dot_general(a, b, (((1,), (1,)), ((), ())), preferred_element_type=F32)


def _softmax(s):
    e = jnp.exp(s - jnp.max(s, axis=-1, keepdims=True))
    return e / jnp.sum(e, axis=-1, keepdims=True)


def _ffn_body(x_ref, g_ref, wg_ref, wu_ref, wo_ref, gf_ref, o_ref, xn_ref, *, final_norm):
    j = pl.program_id(1)

    @pl.when(j == 0)
    def _():
        x = x_ref[...]
        xn_ref[...] = _rms(x, g_ref[...]).astype(BF16)
        o_ref[...] = x

    xn = xn_ref[...]
    hg = _dot(xn, wg_ref[...])
    hu = _dot(xn, wu_ref[...])
    act = (hg * jax.nn.sigmoid(hg) * (0.5 * hu)).astype(BF16)
    o_ref[...] += _dot(act, wo_ref[...])

    if final_norm:
        @pl.when(j == pl.num_programs(1) - 1)
        def _():
            o_ref[...] = _rms(o_ref[...], gf_ref[...])


def _ffn(x, gain, w_in, w_out, g_final, layer, *, final_norm, tm, tf):
    m, d = x.shape
    f = w_out.shape[1]
    nj = f // tf
    vmem = 2 * 2 * tm * d * 4 + tm * d * 2 + 2 * 3 * d * tf * 2 + tm * d * 4 + 3 * tm * tf * 4
    return pl.pallas_call(
        functools.partial(_ffn_body, final_norm=final_norm),
        out_shape=jax.ShapeDtypeStruct((m, d), F32),
        grid=(m // tm, nj),
        in_specs=[
            pl.BlockSpec((tm, d), lambda i, j: (i, 0)),
            pl.BlockSpec((None, 1, d), lambda i, j: (layer, 0, 0)),
            pl.BlockSpec((None, d, tf), lambda i, j: (layer, 0, j)),
            pl.BlockSpec((None, d, tf), lambda i, j: (layer, 0, j + nj)),
            pl.BlockSpec((None, tf, d), lambda i, j: (layer, j, 0)),
            pl.BlockSpec((1, d), lambda i, j: (0, 0)),
        ],
        out_specs=pl.BlockSpec((tm, d), lambda i, j: (i, 0)),
        scratch_shapes=[pltpu.VMEM((tm, d), BF16)],
        compiler_params=_params(("parallel", "arbitrary"), vmem + 4 * 2**20),
        name="ffn",
    )(x, gain, w_in, w_in, w_out, g_final)


def _proj_body(x_ref, g_ref, w_ref, o_ref, xn_ref):
    @pl.when(pl.program_id(1) == 0)
    def _():
        xn_ref[...] = _rms(x_ref[...], g_ref[...]).astype(BF16)

    o_ref[...] = _dot(xn_ref[...], w_ref[...])


def _rms_proj(x, gain, w, layer, *, tm, tn):
    m, d = x.shape
    n = w.shape[2]
    vmem = 2 * tm * d * 4 + tm * d * 2 + 2 * d * tn * 2 + 3 * tm * tn * 4
    return pl.pallas_call(
        _proj_body,
        out_shape=jax.ShapeDtypeStruct((m, n), F32),
        grid=(m // tm, n // tn),
        in_specs=[
            pl.BlockSpec((tm, d), lambda i, j: (i, 0)),
            pl.BlockSpec((None, 1, d), lambda i, j: (layer, 0, 0)),
            pl.BlockSpec((None, d, tn), lambda i, j: (layer, 0, j)),
        ],
        out_specs=pl.BlockSpec((tm, tn), lambda i, j: (i, j)),
        scratch_shapes=[pltpu.VMEM((tm, d), BF16)],
        compiler_params=_params(("parallel", "arbitrary"), vmem + 4 * 2**20),
        name="mix_in_proj",
    )(x, gain, w)


def _memkv_body(x_ref, g_ref, wk_ref, wv_ref, k_ref, v_ref):
    xn = _rms(x_ref[...], g_ref[...]).astype(BF16)
    k_ref[...] = _dot(xn, wk_ref[...])
    v_ref[...] = _dot(xn, wv_ref[...])


def _mem_kv(mem, gain, w_ck, w_cv, *, tm):
    m, d = mem.shape
    depth, _, n = w_ck.shape
    vmem = 2 * tm * d * 4 + tm * d * 2 + 2 * 2 * d * n * 2 + 2 * 2 * tm * n * 4
    out = jax.ShapeDtypeStruct((depth, m, n), F32)
    return pl.pallas_call(
        _memkv_body,
        out_shape=(out, out),
        grid=(depth, m // tm),
        in_specs=[
            pl.BlockSpec((tm, d), lambda l, i: (i, 0)),
            pl.BlockSpec((None, 1, d), lambda l, i: (l, 0, 0)),
            pl.BlockSpec((None, d, n), lambda l, i: (l, 0, 0)),
            pl.BlockSpec((None, d, n), lambda l, i: (l, 0, 0)),
        ],
        out_specs=(pl.BlockSpec((None, tm, n), lambda l, i: (l, i, 0)),
                   pl.BlockSpec((None, tm, n), lambda l, i: (l, i, 0))),
        compiler_params=_params(("parallel", "parallel"), vmem + 4 * 2**20),
        name="mem_kv",
    )(mem, gain, w_ck, w_cv)


def _mla_proj_body(qc_ref, ckv_ref, kr_ref, krp_ref, cos_ref, sin_ref, nq_ref, nkv_ref, wn_ref, wr_ref, wrp_ref,
                   wuk_ref, q_ref, k_ref, ckv_out_ref, kr_out_ref):
    cos = cos_ref[...]
    sin = sin_ref[...]
    qn = _rms(qc_ref[...], nq_ref[...]).astype(BF16)
    q_nope = _dot(qn, wn_ref[...]).astype(BF16)
    q_r = _dot(qn, wr_ref[...])
    q_rp = _dot(qn, wrp_ref[...])
    for h in range(MLA_HEADS):
        q_lat = _dot(q_nope[:, h * MLA_NOPE:(h + 1) * MLA_NOPE], wuk_ref[h])
        q_ref[:, h * QK_WIDTH:h * QK_WIDTH + KV_LORA] = (q_lat * MLA_SCALE).astype(BF16)
        sl = slice(h * ROPE_PAD, (h + 1) * ROPE_PAD)
        rot = q_r[:, sl] * cos + q_rp[:, sl] * sin
        q_ref[:, h * QK_WIDTH + KV_LORA:(h + 1) * QK_WIDTH] = (rot * MLA_SCALE).astype(BF16)
    ckv_n = _rms(ckv_ref[...], nkv_ref[...])
    ckv_out_ref[...] = ckv_n
    k_ref[:, :KV_LORA] = ckv_n.astype(BF16)
    k_rot = kr_ref[...] * cos + krp_ref[...] * sin
    kr_out_ref[...] = k_rot[:, :MLA_ROPE]
    k_ref[:, KV_LORA:] = k_rot.astype(BF16)


def _mla_proj(h, cos, sin, norm_q, norm_kv, w_nope, w_rope, w_ropep, w_ukt, layer, *, tm):
    m = h.shape[0]
    qw = MLA_HEADS * QK_WIDTH
    const = lambda *shape: pl.BlockSpec((None,) + shape, lambda i: (layer,) + (0,) * len(shape))
    vmem = 2 * tm * (Q_LORA + KV_LORA + 4 * ROPE_PAD) * 4 + 2 * tm * (qw + QK_WIDTH) * 2 \
        + 2 * tm * (KV_LORA + LANES) * 4 + 2 * (3 * Q_LORA * 512 + 4 * 128 * 256) * 2 + 8 * tm * 512 * 4
    return pl.pallas_call(
        _mla_proj_body,
        out_shape=(jax.ShapeDtypeStruct((m, qw), BF16), jax.ShapeDtypeStruct((m, QK_WIDTH), BF16),
                   jax.ShapeDtypeStruct((m, KV_LORA), F32), jax.ShapeDtypeStruct((m, MLA_ROPE), F32)),
        grid=(m // tm,),
        in_specs=[
            pl.BlockSpec((tm, Q_LORA), lambda i: (i, COL_QC // Q_LORA)),
            pl.BlockSpec((tm, KV_LORA), lambda i: (i, COL_CKV // KV_LORA)),
            pl.BlockSpec((tm, ROPE_PAD), lambda i: (i, COL_KR // ROPE_PAD)),
            pl.BlockSpec((tm, ROPE_PAD), lambda i: (i, COL_KRP // ROPE_PAD)),
            pl.BlockSpec((tm, ROPE_PAD), lambda i: (i, 0)),
            pl.BlockSpec((tm, ROPE_PAD), lambda i: (i, 0)),
            const(1, Q_LORA), const(1, KV_LORA),
            const(Q_LORA, MLA_HEADS * MLA_NOPE), const(Q_LORA, MLA_HEADS * ROPE_PAD),
            const(Q_LORA, MLA_HEADS * ROPE_PAD), const(MLA_HEADS, MLA_NOPE, KV_LORA),
        ],
        out_specs=(pl.BlockSpec((tm, qw), lambda i: (i, 0)), pl.BlockSpec((tm, QK_WIDTH), lambda i: (i, 0)),
                   pl.BlockSpec((tm, KV_LORA), lambda i: (i, 0)), pl.BlockSpec((tm, MLA_ROPE), lambda i: (i, 0))),
        compiler_params=_params(("parallel",), vmem + 4 * 2**20),
        name="mla_proj",
    )(h, h, h, h, cos, sin, norm_q, norm_kv, w_nope, w_rope, w_ropep, w_ukt)


def _mla_prompt_body(q_ref, k_ref, o_ref, m_ref, l_ref, acc_ref, *, tq):
    qi = pl.program_id(1)
    ki = pl.program_id(2)

    @pl.when(ki == 0)
    def _():
        m_ref[...] = jnp.full(m_ref.shape, NEG, F32)
        l_ref[...] = jnp.zeros(l_ref.shape, F32)
        acc_ref[...] = jnp.zeros(acc_ref.shape, F32)

    @pl.when(ki <= qi)
    def _():
        k = k_ref[...]
        v = k[:, :KV_LORA]
        q_pos = qi * tq + lax.broadcasted_iota(jnp.int32, (tq, tq), 0)
        k_pos = ki * tq + lax.broadcasted_iota(jnp.int32, (tq, tq), 1)
        allowed = k_pos <= q_pos
        for h in range(MLA_HEADS):
            s = _dot_nt(q_ref[:, h * QK_WIDTH:(h + 1) * QK_WIDTH], k)
            s = jnp.where(allowed, s, NEG)
            m_prev = m_ref[h][:, :1]
            l_prev = l_ref[h][:, :1]
            m_new = jnp.maximum(m_prev, jnp.max(s, axis=-1, keepdims=True))
            alpha = jnp.exp(m_prev - m_new)
            p = jnp.exp(s - m_new)
            l_new = alpha * l_prev + jnp.sum(p, axis=-1, keepdims=True)
            acc_ref[h] = alpha * acc_ref[h] + _dot(p.astype(BF16), v)
            m_ref[h] = jnp.broadcast_to(m_new, (tq, LANES))
            l_ref[h] = jnp.broadcast_to(l_new, (tq, LANES))

    @pl.when(ki == qi)
    def _():
        for h in range(MLA_HEADS):
            o_ref[:, h * KV_LORA:(h + 1) * KV_LORA] = (acc_ref[h] / l_ref[h][:, :1]).astype(BF16)


def _mla_prompt(q_full, k_full, n_batch, *, tq):
    m = q_full.shape[0]
    nq = m // n_batch // tq
    qw = MLA_HEADS * QK_WIDTH
    ow = MLA_HEADS * KV_LORA
    vmem = 2 * tq * qw * 2 + 2 * tq * QK_WIDTH * 2 + 2 * tq * ow * 2 \
        + MLA_HEADS * tq * (2 * LANES + KV_LORA) * 4 + 6 * tq * tq * 4
    return pl.pallas_call(
        functools.partial(_mla_prompt_body, tq=tq),
        out_shape=jax.ShapeDtypeStruct((m, ow), BF16),
        grid=(n_batch, nq, nq),
        in_specs=[
            pl.BlockSpec((tq, qw), lambda b, qi, ki: (b * nq + qi, 0)),
            pl.BlockSpec((tq, QK_WIDTH), lambda b, qi, ki: (b * nq + jnp.minimum(ki, qi), 0)),
        ],
        out_specs=pl.BlockSpec((tq, ow), lambda b, qi, ki: (b * nq + qi, 0)),
        scratch_shapes=[pltpu.VMEM((MLA_HEADS, tq, LANES), F32), pltpu.VMEM((MLA_HEADS, tq, LANES), F32),
                        pltpu.VMEM((MLA_HEADS, tq, KV_LORA), F32)],
        compiler_params=_params(("parallel", "parallel", "arbitrary"), vmem + 4 * 2**20),
        name="mla_prompt_attn",
    )(q_full, k_full)


def _ckv_page_copy(ckv_hbm, ckv_buf, sem, layer, page, slot, j, page_size):
    return pltpu.make_async_copy(ckv_hbm.at[layer, page], ckv_buf.at[slot, pl.ds(j * page_size, page_size)],
                                 sem.at[slot])


def _krt_page_copy(krt_hbm, krt_buf, sem, layer, page, slot, j, page_size):
    return pltpu.make_async_copy(krt_hbm.at[layer, page], krt_buf.at[slot, :, pl.ds(j * page_size, page_size)],
                                 sem.at[slot])


def _mla_sample_body(pt_ref, q_ref, kn_ref, ckv_hbm, krt_hbm, o_ref, ckv_buf, krt_buf, kbf_ref, p_ref, onew_ref,
                     sem_c, sem_r, *, layer, n_pages, page_size, chunk, t_new):
    n = pl.program_id(0)
    n_samples = pl.num_programs(0) - 1
    past = n_pages * page_size

    def start_fetch(sample, slot):
        for j in range(n_pages):
            page = pt_ref[sample, j]
            _ckv_page_copy(ckv_hbm, ckv_buf, sem_c, layer, page, slot, j, page_size).start()
            _krt_page_copy(krt_hbm, krt_buf, sem_r, layer, page, slot, j, page_size).start()

    def wait_fetch(slot):
        for j in range(n_pages):
            _ckv_page_copy(ckv_hbm, ckv_buf, sem_c, layer, 0, slot, j, page_size).wait()
            _krt_page_copy(krt_hbm, krt_buf, sem_r, layer, 0, slot, j, page_size).wait()

    slot = lax.rem(n, 2)

    @pl.when(n == 0)
    def _():
        start_fetch(0, 0)
        kbf_ref[1] = jnp.zeros(kbf_ref.shape[1:], BF16)
        p_ref[...] = jnp.zeros(p_ref.shape, BF16)
        onew_ref[...] = jnp.zeros(onew_ref.shape, F32)

    @pl.when(n + 1 < n_samples)
    def _():
        start_fetch(n + 1, 1 - slot)

    @pl.when(n < n_samples)
    def _():
        wait_fetch(slot)

    def step(cur):
        prev = 1 - cur
        o_ref[...] = (_dot(p_ref[...], kbf_ref[prev]) + onew_ref[...]).astype(BF16)

        q = q_ref[...]
        q_lat = q[:, :KV_LORA]
        q_rope = q[:, KV_LORA:KV_LORA + MLA_ROPE]
        rows = q.shape[0]
        for c in range(past // chunk):
            sl = pl.ds(c * chunk, chunk)
            kbf_ref[cur, sl, :] = ckv_buf[cur, sl, :].astype(BF16)
        s_past = _dot_nt(q_lat, kbf_ref[cur]) + _dot(q_rope, krt_buf[cur].astype(BF16))

        kn = kn_ref[...]
        s_new = _dot_nt(q, kn)
        q_tok = lax.broadcasted_iota(jnp.int32, (rows, t_new), 0) // MLA_HEADS
        k_tok = lax.broadcasted_iota(jnp.int32, (rows, t_new), 1)
        s_new = jnp.where(k_tok <= q_tok, s_new, NEG)

        m = jnp.maximum(jnp.max(s_past, axis=-1, keepdims=True), jnp.max(s_new, axis=-1, keepdims=True))
        p_past = jnp.exp(s_past - m)
        p_new = jnp.exp(s_new - m)
        denom = jnp.sum(p_past, axis=-1, keepdims=True) + jnp.sum(p_new, axis=-1, keepdims=True)
        p_ref[...] = (p_past / denom).astype(BF16)
        onew_ref[...] = _dot((p_new / denom).astype(BF16), kn[:, :KV_LORA])

    @pl.when(slot == 0)
    def _():
        step(0)

    @pl.when(slot == 1)
    def _():
        step(1)


def _mla_sample(page_table, q3, kn3, cache_ckv, cache_krope_t, layer, *, chunk):
    n_samples, rows, _ = q3.shape
    t_new = kn3.shape[1]
    n_pages = page_table.shape[1]
    page_size = cache_ckv.shape[2]
    past = n_pages * page_size
    vmem = 2 * past * (KV_LORA + MLA_ROPE) * 4 + past * (2 * KV_LORA + MLA_ROPE) * 2 + rows * past * 4 * 5 \
        + 2 * chunk * KV_LORA * 4
    cur = lambda n, pt: (jnp.minimum(n, n_samples - 1), 0, 0)
    grid_spec = pltpu.PrefetchScalarGridSpec(
        num_scalar_prefetch=1,
        grid=(n_samples + 1,),
        in_specs=[
            pl.BlockSpec((None, rows, QK_WIDTH), cur),
            pl.BlockSpec((None, t_new, QK_WIDTH), cur),
            pl.BlockSpec(memory_space=pl.ANY),
            pl.BlockSpec(memory_space=pl.ANY),
        ],
        out_specs=pl.BlockSpec((None, rows, KV_LORA), lambda n, pt: (jnp.maximum(n - 1, 0), 0, 0)),
        scratch_shapes=[
            pltpu.VMEM((2, past, KV_LORA), F32),
            pltpu.VMEM((2, MLA_ROPE, past), F32),
            pltpu.VMEM((2, past, KV_LORA), BF16),
            pltpu.VMEM((rows, past), BF16),
            pltpu.VMEM((rows, KV_LORA), F32),
            pltpu.SemaphoreType.DMA((2,)),
            pltpu.SemaphoreType.DMA((2,)),
        ],
    )
    return pl.pallas_call(
        functools.partial(_mla_sample_body, layer=layer, n_pages=n_pages, page_size=page_size, chunk=chunk,
                          t_new=t_new),
        out_shape=jax.ShapeDtypeStruct((n_samples, rows, KV_LORA), BF16),
        grid_spec=grid_spec,
        compiler_params=_params(("arbitrary",), vmem + 4 * 2**20),
        name="mla_sample_attn",
    )(page_table, q3, kn3, cache_ckv, cache_krope_t)


def _sg_body(h_ref, w_ref, b_ref, g_ref, beta_ref, y_ref, *v_refs, period, tm):
    row = lax.broadcasted_iota(jnp.int32, (CHUNK, CHUNK), 0)
    col = lax.broadcasted_iota(jnp.int32, (CHUNK, CHUNK), 1)
    shift = period.bit_length() - 1
    keep = (col <= row) & ((row >> shift) == (col >> shift))
    w = [jnp.where(keep, w_ref[h], 0.0).astype(BF16) for h in range(SG_HEADS)]
    bias = b_ref[...]
    for c in range(tm // CHUNK):
        rows = pl.ds(c * CHUNK, CHUNK)
        a = jax.nn.gelu(h_ref[rows, :])
        u = a[:, :GROUP_WIDTH]
        v = _ln(a[:, GROUP_WIDTH:], g_ref[...], beta_ref[...])
        if v_refs:
            v_refs[0][rows, :] = v
        vb = v.astype(BF16)
        for h in range(SG_HEADS):
            sl = slice(h * SG_HEAD_DIM, (h + 1) * SG_HEAD_DIM)
            g = _dot(w[h], vb[:, sl]) + bias[:, sl]
            y_ref[rows, sl] = (u[:, sl] * g).astype(BF16)


def _spatial_gate(h, w, b_rows, ln_g, ln_b, layer, *, period, with_v, tm):
    m = h.shape[0]
    const = lambda *shape: pl.BlockSpec((None,) + shape, lambda i: (layer,) + (0,) * len(shape))
    out_shape = [jax.ShapeDtypeStruct((m, GROUP_WIDTH), BF16)]
    out_specs = [pl.BlockSpec((tm, GROUP_WIDTH), lambda i: (i, 0))]
    if with_v:
        out_shape.append(jax.ShapeDtypeStruct((m, GROUP_WIDTH), F32))
        out_specs.append(pl.BlockSpec((tm, GROUP_WIDTH), lambda i: (i, 0)))
    vmem = 2 * tm * 2 * GROUP_WIDTH * 4 + 2 * tm * GROUP_WIDTH * 6 + 16 * CHUNK * 2 * GROUP_WIDTH * 4
    return pl.pallas_call(
        functools.partial(_sg_body, period=period, tm=tm),
        out_shape=tuple(out_shape),
        grid=(m // tm,),
        in_specs=[
            pl.BlockSpec((tm, 2 * GROUP_WIDTH), lambda i: (i, COL_B // (2 * GROUP_WIDTH))),
            const(SG_HEADS, CHUNK, CHUNK), const(CHUNK, GROUP_WIDTH), const(1, GROUP_WIDTH), const(1, GROUP_WIDTH),
        ],
        out_specs=tuple(out_specs),
        compiler_params=_params(("parallel",), vmem + 4 * 2**20),
        name="spatial_gate",
    )(h, w, b_rows, ln_g, ln_b)


def _conv_tail(y, ln_g, ln_b):
    return jax.nn.silu(_ln(y, ln_g, ln_b))


def _pc_prompt_body(hc_ref, hd_ref, pw_ref, ps_ref, cw_ref, cb_ref, g_ref, beta_ref,
                    yc_ref, yd_ref, pool_out_ref, conv_out_ref, cext, zext, zsh, ybuf, *, tl, rb):
    li = pl.program_id(1)

    @pl.when(li == 0)
    def _():
        cext[0:HALO_POOL, :] = jnp.zeros((HALO_POOL, GROUP_WIDTH), F32)
        zext[0:HALO_CONV, :] = jnp.zeros((HALO_CONV, GROUP_WIDTH), F32)

    hc = hc_ref[...]
    cext[HALO_POOL:HALO_POOL + tl, :] = hc
    pos = li * tl + lax.broadcasted_iota(jnp.int32, (tl, 1), 0)
    for gi, win in enumerate(POOL_WINDOWS):
        sl = slice(gi * POOL_GROUP, (gi + 1) * POOL_GROUP)
        acc = hc[:, sl]
        for k in range(1, win):
            acc = acc + cext[pl.ds(HALO_POOL - k, tl), sl]
        cnt = jnp.minimum(pos + 1, win).astype(F32)
        pooled = acc / cnt - hc[:, sl]
        y = _dot(pooled.astype(BF16), pw_ref[gi]) * ps_ref[:, sl]
        yc_ref[:, sl] = y.astype(BF16)

    hd = hd_ref[...]
    zext[HALO_CONV:HALO_CONV + tl, :] = hd[:, :GROUP_WIDTH] * jax.nn.sigmoid(hd[:, GROUP_WIDTH:])
    first = HALO_CONV - CONV_STATE
    for b in range(SUBLANES):
        span = tl + (CONV_WIDTH - 1 - b) // SUBLANES * SUBLANES
        zsh[b, 0:span, :] = zext[pl.ds(first + b, span), :]
    for r0 in range(0, tl, rb):
        for cg in range(GROUP_WIDTH // LANES):
            sl = slice(cg * LANES, (cg + 1) * LANES)
            acc = jnp.broadcast_to(cb_ref[:, sl], (rb, LANES))
            for k in range(CONV_WIDTH):
                acc = acc + cw_ref[k:k + 1, sl] * zsh[k % SUBLANES, pl.ds(r0 + k // SUBLANES * SUBLANES, rb), sl]
            ybuf[r0:r0 + rb, sl] = acc
    yd_ref[...] = _conv_tail(ybuf[...], g_ref[...], beta_ref[...]).astype(BF16)

    @pl.when(li == pl.num_programs(1) - 1)
    def _():
        pool_out_ref[...] = cext[pl.ds(HALO_POOL + tl - POOL_STATE, POOL_STATE), :]
        conv_out_ref[...] = zext[pl.ds(HALO_CONV + tl - CONV_STATE, CONV_STATE), :]

    cext[0:HALO_POOL, :] = cext[tl:tl + HALO_POOL, :]
    zext[0:HALO_CONV, :] = zext[tl:tl + HALO_CONV, :]


def _pool_conv_prompt(h, n_batch, pool_w, pool_scale, conv_w, conv_b, ln_g, ln_b, layer, *, tl, rb):
    m = h.shape[0]
    nl = m // n_batch // tl
    const = lambda *shape: pl.BlockSpec((None,) + shape, lambda b, l: (layer,) + (0,) * len(shape))
    y = jax.ShapeDtypeStruct((m, GROUP_WIDTH), BF16)
    vmem = 2 * tl * 3 * GROUP_WIDTH * 4 + 2 * 2 * tl * GROUP_WIDTH * 2 + 3 * (tl + HALO_CONV) * GROUP_WIDTH * 4 \
        + 8 * tl * GROUP_WIDTH * 4
    return pl.pallas_call(
        functools.partial(_pc_prompt_body, tl=tl, rb=rb),
        out_shape=(y, y, jax.ShapeDtypeStruct((n_batch, POOL_STATE, GROUP_WIDTH), F32),
                   jax.ShapeDtypeStruct((n_batch, CONV_STATE, GROUP_WIDTH), F32)),
        grid=(n_batch, nl),
        in_specs=[
            pl.BlockSpec((tl, GROUP_WIDTH), lambda b, l: (b * nl + l, COL_C // GROUP_WIDTH)),
            pl.BlockSpec((tl, 2 * GROUP_WIDTH), lambda b, l: (b * nl + l, COL_D // (2 * GROUP_WIDTH))),
            const(len(POOL_WINDOWS), POOL_GROUP, POOL_GROUP), const(1, GROUP_WIDTH),
            const(CONV_WIDTH, GROUP_WIDTH), const(1, GROUP_WIDTH), const(1, GROUP_WIDTH), const(1, GROUP_WIDTH),
        ],
        out_specs=(pl.BlockSpec((tl, GROUP_WIDTH), lambda b, l: (b * nl + l, 0)),
                   pl.BlockSpec((tl, GROUP_WIDTH), lambda b, l: (b * nl + l, 0)),
                   pl.BlockSpec((None, POOL_STATE, GROUP_WIDTH), lambda b, l: (b, 0, 0)),
                   pl.BlockSpec((None, CONV_STATE, GROUP_WIDTH), lambda b, l: (b, 0, 0))),
        scratch_shapes=[pltpu.VMEM((HALO_POOL + tl, GROUP_WIDTH), F32), pltpu.VMEM((HALO_CONV + tl, GROUP_WIDTH), F32),
                        pltpu.VMEM((SUBLANES, tl + HALO_CONV - SUBLANES, GROUP_WIDTH), F32),
                        pltpu.VMEM((tl, GROUP_WIDTH), F32)],
        compiler_params=_params(("parallel", "arbitrary"), vmem + SUBLANES * (tl + HALO_CONV) * GROUP_WIDTH * 4 + 4 * 2**20),
        name="pool_conv_prompt",
    )(h, h, pool_w, pool_scale, conv_w, conv_b, ln_g, ln_b)


def _pc_sample_body(hc_ref, hd_ref, sp_ref, sc_ref, pw_ref, ps_ref, cw_ref, cb_ref, g_ref, beta_ref,
                    yc_ref, yd_ref, pool_out_ref, conv_out_ref, cext, zext, ybuf, *, bs, t_new, past_len):
    rows = bs * t_new
    hc = hc_ref[...]
    cext[:, 0:POOL_STATE, :] = sp_ref[...]
    cext[:, POOL_STATE:POOL_STATE + t_new, :] = hc.reshape(bs, t_new, GROUP_WIDTH)
    pos = past_len + lax.broadcasted_iota(jnp.int32, (1, t_new, 1), 1)
    for gi, win in enumerate(POOL_WINDOWS):
        sl = slice(gi * POOL_GROUP, (gi + 1) * POOL_GROUP)
        acc = cext[:, pl.ds(POOL_STATE, t_new), sl]
        for k in range(1, win):
            acc = acc + cext[:, pl.ds(POOL_STATE - k, t_new), sl]
        cnt = jnp.minimum(pos + 1, win).astype(F32)
        pooled = (acc / cnt).reshape(rows, POOL_GROUP) - hc[:, sl]
        y = _dot(pooled.astype(BF16), pw_ref[gi]) * ps_ref[:, sl]
        yc_ref[:, sl] = y.astype(BF16)
    pool_out_ref[...] = cext[:, pl.ds(t_new, POOL_STATE), :]

    hd = hd_ref[...]
    z = hd[:, :GROUP_WIDTH] * jax.nn.sigmoid(hd[:, GROUP_WIDTH:])
    zext[:, 0:CONV_STATE, :] = sc_ref[...]
    zext[:, CONV_STATE:CONV_STATE + t_new, :] = z.reshape(bs, t_new, GROUP_WIDTH)
    for cg in range(GROUP_WIDTH // LANES):
        sl = slice(cg * LANES, (cg + 1) * LANES)
        acc = jnp.broadcast_to(cb_ref[:, sl].reshape(1, 1, LANES), (bs, t_new, LANES))
        for k in range(CONV_WIDTH):
            acc = acc + cw_ref[k:k + 1, sl].reshape(1, 1, LANES) * zext[:, pl.ds(k, t_new), sl]
        ybuf[:, sl] = acc.reshape(rows, LANES)
    yd_ref[...] = _conv_tail(ybuf[...], g_ref[...], beta_ref[...]).astype(BF16)
    conv_out_ref[...] = zext[:, pl.ds(t_new, CONV_STATE), :]


def _pool_conv_sample(h, state_pool, state_conv, pool_w, pool_scale, conv_w, conv_b, ln_g, ln_b, layer, *,
                      bs, t_new, past_len):
    m = h.shape[0]
    n_samples = m // t_new
    rows = bs * t_new
    const = lambda *shape: pl.BlockSpec((None,) + shape, lambda i: (layer,) + (0,) * len(shape))
    y = jax.ShapeDtypeStruct((m, GROUP_WIDTH), BF16)
    pool_rows = POOL_STATE + t_new + 1
    conv_rows = CONV_STATE + t_new + 2
    vmem = 2 * rows * 3 * GROUP_WIDTH * 4 + 4 * bs * (16 + 32) * GROUP_WIDTH * 4 \
        + bs * (pool_rows + conv_rows) * GROUP_WIDTH * 4 + 8 * rows * GROUP_WIDTH * 4
    return pl.pallas_call(
        functools.partial(_pc_sample_body, bs=bs, t_new=t_new, past_len=past_len),
        out_shape=(y, y, jax.ShapeDtypeStruct((n_samples, POOL_STATE, GROUP_WIDTH), F32),
                   jax.ShapeDtypeStruct((n_samples, CONV_STATE, GROUP_WIDTH), F32)),
        grid=(n_samples // bs,),
        in_specs=[
            pl.BlockSpec((rows, GROUP_WIDTH), lambda i: (i, COL_C // GROUP_WIDTH)),
            pl.BlockSpec((rows, 2 * GROUP_WIDTH), lambda i: (i, COL_D // (2 * GROUP_WIDTH))),
            pl.BlockSpec((None, bs, POOL_STATE, GROUP_WIDTH), lambda i: (layer, i, 0, 0)),
            pl.BlockSpec((None, bs, CONV_STATE, GROUP_WIDTH), lambda i: (layer, i, 0, 0)),
            const(len(POOL_WINDOWS), POOL_GROUP, POOL_GROUP), const(1, GROUP_WIDTH),
            const(CONV_WIDTH, GROUP_WIDTH), const(1, GROUP_WIDTH), const(1, GROUP_WIDTH), const(1, GROUP_WIDTH),
        ],
        out_specs=(pl.BlockSpec((rows, GROUP_WIDTH), lambda i: (i, 0)),
                   pl.BlockSpec((rows, GROUP_WIDTH), lambda i: (i, 0)),
                   pl.BlockSpec((bs, POOL_STATE, GROUP_WIDTH), lambda i: (i, 0, 0)),
                   pl.BlockSpec((bs, CONV_STATE, GROUP_WIDTH), lambda i: (i, 0, 0))),
        scratch_shapes=[pltpu.VMEM((bs, pool_rows, GROUP_WIDTH), F32), pltpu.VMEM((bs, conv_rows, GROUP_WIDTH), F32),
                        pltpu.VMEM((rows, GROUP_WIDTH), F32)],
        compiler_params=_params(("parallel",), vmem + 4 * 2**20),
        name="pool_conv_sample",
    )(h, h, state_pool, state_conv, pool_w, pool_scale, conv_w, conv_b, ln_g, ln_b)


def _mix_out_value(x_ref, ol_ref, yb_ref, yc_ref, yd_ref, wuv_ref, wo_ref):
    ol = ol_ref[...]
    parts = [_dot(ol[:, h * KV_LORA:(h + 1) * KV_LORA], wuv_ref[h]).astype(BF16) for h in range(MLA_HEADS)]
    y = jnp.concatenate(parts + [yb_ref[...], yc_ref[...], yd_ref[...]], axis=1)
    return x_ref[...] + _dot(y, wo_ref[...])


def _mix_out_body(x_ref, ol_ref, yb_ref, yc_ref, yd_ref, wuv_ref, wo_ref, o_ref):
    o_ref[...] = _mix_out_value(x_ref, ol_ref, yb_ref, yc_ref, yd_ref, wuv_ref, wo_ref)


def _mix_out(x, o_lat, y_b, y_c, y_d, w_uv, w_out, layer, *, tm):
    m, d = x.shape
    mw = w_out.shape[1]
    row = lambda w: pl.BlockSpec((tm, w), lambda i: (i, 0))
    vmem = 2 * 2 * tm * d * 4 + 2 * tm * (MLA_HEADS * KV_LORA + 3 * GROUP_WIDTH) * 2 + 2 * mw * d * 2 \
        + tm * mw * 2 + tm * d * 4
    return pl.pallas_call(
        _mix_out_body,
        out_shape=jax.ShapeDtypeStruct((m, d), F32),
        grid=(m // tm,),
        in_specs=[
            row(d), row(MLA_HEADS * KV_LORA), row(GROUP_WIDTH), row(GROUP_WIDTH), row(GROUP_WIDTH),
            pl.BlockSpec((None, MLA_HEADS, KV_LORA, 128), lambda i: (layer, 0, 0, 0)),
            pl.BlockSpec((None, mw, d), lambda i: (layer, 0, 0)),
        ],
        out_specs=row(d),
        compiler_params=_params(("parallel",), vmem + 4 * 2**20),
        name="mix_out_proj",
    )(x, o_lat, y_b, y_c, y_d, w_uv, w_out)


def _cross_prompt_value(x, g_ref, wq_ref, wo_ref, mk_ref, mv_ref):
    xn = _rms(x, g_ref[...]).astype(BF16)
    q = (_dot(xn, wq_ref[...]) * MEM_HEAD_DIM ** -0.5).astype(BF16)
    k = mk_ref[...].astype(BF16)
    v = mv_ref[...].astype(BF16)
    heads = []
    for h in range(MEM_HEADS):
        sl = slice(h * MEM_HEAD_DIM, (h + 1) * MEM_HEAD_DIM)
        p = _softmax(_dot_nt(q[:, sl], k[:, sl]))
        heads.append(_dot(p.astype(BF16), v[:, sl]).astype(BF16))
    return x + _dot(jnp.concatenate(heads, axis=1), wo_ref[...])


def _mix_cross_prompt_body(x_ref, ol_ref, yb_ref, yc_ref, yd_ref, wuv_ref, wout_ref, g_ref, wq_ref, wo_ref,
                           mk_ref, mv_ref, o_ref):
    x1 = _mix_out_value(x_ref, ol_ref, yb_ref, yc_ref, yd_ref, wuv_ref, wout_ref)
    o_ref[...] = _cross_prompt_value(x1, g_ref, wq_ref, wo_ref, mk_ref, mv_ref)


def _mix_cross_prompt(x, o_lat, y_b, y_c, y_d, w_uv, w_out, gain, w_cq, w_co, mem_k, mem_v, layer, *, tq):
    m, d = x.shape
    mw = w_out.shape[1]
    n_batch, mem_len = mem_k.shape[1:3]
    nq = m // n_batch // tq
    row = lambda w: pl.BlockSpec((tq, w), lambda b, i: (b * nq + i, 0))
    const = lambda *shape: pl.BlockSpec((None,) + shape, lambda b, i: (layer,) + (0,) * len(shape))
    mem = pl.BlockSpec((None, None, mem_len, MEM_WIDTH), lambda b, i: (layer, b, 0, 0))
    vmem = 2 * 2 * tq * d * 4 + 2 * tq * (MLA_HEADS * KV_LORA + 3 * GROUP_WIDTH) * 2 + 2 * mw * d * 2 \
        + 2 * 2 * d * MEM_WIDTH * 2 + 2 * 2 * mem_len * MEM_WIDTH * 4 + tq * mw * 2 + tq * d * 10 + 8 * tq * MEM_WIDTH * 4
    return pl.pallas_call(
        _mix_cross_prompt_body,
        out_shape=jax.ShapeDtypeStruct((m, d), F32),
        grid=(n_batch, nq),
        in_specs=[
            row(d), row(MLA_HEADS * KV_LORA), row(GROUP_WIDTH), row(GROUP_WIDTH), row(GROUP_WIDTH),
            const(MLA_HEADS, KV_LORA, 128), const(mw, d),
            const(1, d), const(d, MEM_WIDTH), const(MEM_WIDTH, d), mem, mem,
        ],
        out_specs=row(d),
        compiler_params=_params(("parallel", "parallel"), vmem + 4 * 2**20),
        name="mix_cross_prompt",
    )(x, o_lat, y_b, y_c, y_d, w_uv, w_out, gain, w_cq, w_co, mem_k, mem_v)


def _cross_sample_body(x_ref, g_ref, wq_ref, wo_ref, mk_ref, mv_ref, o_ref, attn_ref, *, bs, t_new):
    x = x_ref[...]
    xn = _rms(x, g_ref[...]).astype(BF16)
    q = _dot(xn, wq_ref[...]) * MEM_HEAD_DIM ** -0.5
    rows = MEM_HEADS * t_new
    lane_head = lax.broadcasted_iota(jnp.int32, (rows, MEM_WIDTH), 1) // MEM_HEAD_DIM
    row_head = lax.broadcasted_iota(jnp.int32, (rows, MEM_WIDTH), 0) // t_new
    own = lane_head == row_head
    scores = []
    for s in range(bs):
        qs = q[s * t_new:(s + 1) * t_new, :]
        q_bd = jnp.where(own, jnp.concatenate([qs] * MEM_HEADS, axis=0), 0.0).astype(BF16)
        scores.append(_dot_nt(q_bd, mk_ref[s].astype(BF16)))
    p = _softmax(jnp.concatenate(scores, axis=0)).astype(BF16)
    for s in range(bs):
        r = jnp.where(own, _dot(p[s * rows:(s + 1) * rows], mv_ref[s].astype(BF16)), 0.0)
        o = r[0:t_new]
        for h in range(1, MEM_HEADS):
            o = o + r[h * t_new:(h + 1) * t_new]
        attn_ref[s * t_new:(s + 1) * t_new, :] = o
    o_ref[...] = x + _dot(attn_ref[...].astype(BF16), wo_ref[...])


def _cross_sample(x, gain, w_cq, w_co, mem_k, mem_v, layer, *, bs, t_new):
    m, d = x.shape
    mem_len = mem_k.shape[2]
    rows = bs * t_new
    const = lambda *shape: pl.BlockSpec((None,) + shape, lambda i: (layer,) + (0,) * len(shape))
    vmem = 2 * 2 * rows * d * 4 + 2 * 2 * d * MEM_WIDTH * 2 + 2 * 2 * bs * mem_len * MEM_WIDTH * 4 \
        + rows * d * 6 + 16 * mem_len * MEM_WIDTH * 4
    return pl.pallas_call(
        functools.partial(_cross_sample_body, bs=bs, t_new=t_new),
        out_shape=jax.ShapeDtypeStruct((m, d), F32),
        grid=(m // rows,),
        in_specs=[
            pl.BlockSpec((rows, d), lambda i: (i, 0)),
            const(1, d), const(d, MEM_WIDTH), const(MEM_WIDTH, d),
            pl.BlockSpec((None, bs, mem_len, MEM_WIDTH), lambda i: (layer, i, 0, 0)),
            pl.BlockSpec((None, bs, mem_len, MEM_WIDTH), lambda i: (layer, i, 0, 0)),
        ],
        out_specs=pl.BlockSpec((rows, d), lambda i: (i, 0)),
        scratch_shapes=[pltpu.VMEM((rows, MEM_WIDTH), F32)],
        compiler_params=_params(("parallel",), vmem + 4 * 2**20),
        name="cross_attn_sample",
    )(x, gain, w_cq, w_co, mem_k, mem_v)


def _rotate_half_cols(w):
    half = w.shape[-1] // 2
    return jnp.concatenate([-w[..., half:], w[..., :half]], axis=-1)


def _pad_cols(w, width):
    return jnp.pad(w, [(0, 0)] * (w.ndim - 1) + [(0, width - w.shape[-1])])


def _rope_tables(pos):
    half = MLA_ROPE // 2
    inv = ROPE_THETA ** (-jnp.arange(half, dtype=F32) / half)
    ang = pos.astype(F32)[:, None] * inv[None, :]
    c = jnp.cos(ang)
    s = jnp.sin(ang)
    return (_pad_cols(jnp.concatenate([c, c], axis=1), ROPE_PAD), _pad_cols(jnp.concatenate([s, s], axis=1), ROPE_PAD))


def _pick_tile(m, pref):
    t = min(m, pref)
    while m % t:
        t //= 2
    return t


def kernel(x_prompt, x_sample, mem_prompt, cache_ckv, cache_krope, cache_mem_k, cache_mem_v, state_pool, state_conv, page_table, w_in, w_out, norm_q_lat, w_qb, norm_kv_lat, w_uk, w_uv, sg_w, sg_b, sg_ln_g, sg_ln_b, pool_w, pool_scale, conv_w, conv_b, conv_ln_g, conv_ln_b, norm_mix, norm_ffn_a, w_ffn_a_in, w_ffn_a_out, norm_ffn_b, w_ffn_b_in, w_ffn_b_out, norm_cross, norm_mem, w_cq, w_ck, w_cv, w_co, norm_final):
    n_b, seq, d = x_prompt.shape
    n_db, t_new, _ = x_sample.shape
    depth = w_in.shape[0]
    n_pages = page_table.shape[1]
    page_size = cache_ckv.shape[2]
    past_len = n_pages * page_size
    mem_len = mem_prompt.shape[1]

    a_cols = Q_LORA + KV_LORA + MLA_ROPE
    b0, c0, d0 = a_cols, a_cols + 2 * GROUP_WIDTH, a_cols + 3 * GROUP_WIDTH
    w_kr = w_in[:, :, Q_LORA + KV_LORA:a_cols]
    w_in_r = jnp.concatenate([
        w_in[:, :, b0:c0], w_in[:, :, d0:], w_in[:, :, :Q_LORA], w_in[:, :, c0:d0],
        w_in[:, :, Q_LORA:Q_LORA + KV_LORA], _pad_cols(w_kr, ROPE_PAD), _pad_cols(_rotate_half_cols(w_kr), ROPE_PAD),
    ], axis=-1).astype(BF16)
    w_q_nope = w_qb[..., :MLA_NOPE].reshape(depth, Q_LORA, MLA_HEADS * MLA_NOPE).astype(BF16)
    w_q_r = w_qb[..., MLA_NOPE:]
    w_q_rope = _pad_cols(w_q_r, ROPE_PAD).reshape(depth, Q_LORA, MLA_HEADS * ROPE_PAD).astype(BF16)
    w_q_ropep = _pad_cols(_rotate_half_cols(w_q_r), ROPE_PAD).reshape(depth, Q_LORA, MLA_HEADS * ROPE_PAD).astype(BF16)
    w_ukt = jnp.transpose(w_uk, (0, 2, 3, 1)).astype(BF16)
    w_uvh = jnp.transpose(w_uv, (0, 2, 1, 3)).astype(BF16)
    w_out_b = w_out.astype(BF16)
    w_fa_in, w_fa_out = w_ffn_a_in.astype(BF16), w_ffn_a_out.astype(BF16)
    w_fb_in, w_fb_out = w_ffn_b_in.astype(BF16), w_ffn_b_out.astype(BF16)
    w_cq_b, w_ck_b, w_cv_b, w_co_b = (w.astype(BF16) for w in (w_cq, w_ck, w_cv, w_co))
    pool_w_b = pool_w.astype(BF16)
    row3 = lambda g: g.reshape(depth, 1, g.shape[-1])
    n_mix, n_fa, n_fb, n_cr, n_mem = (row3(g) for g in (norm_mix, norm_ffn_a, norm_ffn_b, norm_cross, norm_mem))
    n_q, n_kv = row3(norm_q_lat), row3(norm_kv_lat)
    sg_g, sg_beta = row3(sg_ln_g), row3(sg_ln_b)
    p_scale, c_b, c_g, c_beta = row3(pool_scale), row3(conv_b), row3(conv_ln_g), row3(conv_ln_b)
    n_fin = norm_final.reshape(1, d)
    reps = CHUNK // t_new
    sg_w_p = sg_w
    sg_w_s = jnp.tile(sg_w[:, :, :t_new, :t_new], (1, 1, reps, reps))
    bias_rows = lambda b: jnp.repeat(jnp.transpose(b, (0, 2, 1)), SG_HEAD_DIM, axis=2)
    sg_b_p = bias_rows(sg_b)
    sg_b_s = jnp.tile(bias_rows(sg_b[:, :, :t_new]), (1, reps, 1))

    cache_krope_t = jnp.swapaxes(cache_krope, 2, 3)

    cos_p, sin_p = _rope_tables(jnp.tile(jnp.arange(seq), n_b))
    cos_s, sin_s = _rope_tables(jnp.tile(past_len + jnp.arange(t_new), n_db))

    mp = n_b * seq
    ms = n_db * t_new
    xp = x_prompt.reshape(mp, d)
    xs = x_sample.reshape(ms, d)

    mem_k, mem_v = _mem_kv(mem_prompt.reshape(n_b * mem_len, d), n_mem, w_ck_b, w_cv_b, tm=_pick_tile(n_b * mem_len, 512))
    mem_k = mem_k.reshape(depth, n_b, mem_len, MEM_WIDTH)
    mem_v = mem_v.reshape(depth, n_b, mem_len, MEM_WIDTH)

    tm_p = _pick_tile(mp, 512)
    tm_s = _pick_tile(ms, 512)
    tmf_p = _pick_tile(mp, 1024)
    tmf_s = _pick_tile(ms, 1024)
    tf = 512
    bs_pc = _pick_tile(n_db, 16)
    bs_cr = _pick_tile(n_db, 8)
    ckv_p, kr_p, pool_p, conv_p = [], [], [], []
    ckv_s, kr_s, sgv_s, pool_s, conv_s = [], [], [], [], []
    for l in range(depth):
        last = l == depth - 1
        xp = _ffn(xp, n_fa, w_fa_in, w_fa_out, n_fin, l, final_norm=False, tm=tmf_p, tf=tf)
        h = _rms_proj(xp, n_mix, w_in_r, l, tm=_pick_tile(mp, 1024), tn=1792)
        q_full, k_full, ckv, kr = _mla_proj(h, cos_p, sin_p, n_q, n_kv, w_q_nope, w_q_rope, w_q_ropep, w_ukt, l, tm=tm_p)
        o_lat = _mla_prompt(q_full, k_full, n_b, tq=_pick_tile(seq, 512))
        (y_b,) = _spatial_gate(h, sg_w_p, sg_b_p, sg_g, sg_beta, l, period=CHUNK, with_v=False, tm=tm_p)
        y_c, y_d, pst, cst = _pool_conv_prompt(h, n_b, pool_w_b, p_scale, conv_w, c_b, c_g, c_beta, l,
                                               tl=_pick_tile(seq, 256), rb=128)
        xp = _mix_cross_prompt(xp, o_lat, y_b, y_c, y_d, w_uvh, w_out_b, n_cr, w_cq_b, w_co_b, mem_k, mem_v, l,
                               tq=_pick_tile(seq, 512))
        xp = _ffn(xp, n_fb, w_fb_in, w_fb_out, n_fin, l, final_norm=last, tm=tmf_p, tf=tf)
        ckv_p.append(ckv); kr_p.append(kr); pool_p.append(pst); conv_p.append(cst)
        xs = _ffn(xs, n_fa, w_fa_in, w_fa_out, n_fin, l, final_norm=False, tm=tmf_s, tf=tf)
        h = _rms_proj(xs, n_mix, w_in_r, l, tm=_pick_tile(ms, 1024), tn=1792)
        q_full, k_full, ckv, kr = _mla_proj(h, cos_s, sin_s, n_q, n_kv, w_q_nope, w_q_rope, w_q_ropep, w_ukt, l, tm=tm_s)
        o_lat = _mla_sample(page_table, q_full.reshape(n_db, t_new * MLA_HEADS, QK_WIDTH),
                            k_full.reshape(n_db, t_new, QK_WIDTH), cache_ckv, cache_krope_t, l,
                            chunk=_pick_tile(past_len, 1024))
        o_lat = o_lat.reshape(ms, MLA_HEADS * KV_LORA)
        y_b, v_b = _spatial_gate(h, sg_w_s, sg_b_s, sg_g, sg_beta, l, period=t_new, with_v=True, tm=tm_s)
        y_c, y_d, pst, cst = _pool_conv_sample(h, state_pool, state_conv, pool_w_b, p_scale, conv_w, c_b, c_g, c_beta,
                                               l, bs=bs_pc, t_new=t_new, past_len=past_len)
        xs = _mix_out(xs, o_lat, y_b, y_c, y_d, w_uvh, w_out_b, l, tm=tm_s)
        xs = _cross_sample(xs, n_cr, w_cq_b, w_co_b, cache_mem_k, cache_mem_v, l, bs=bs_cr, t_new=t_new)
        xs = _ffn(xs, n_fb, w_fb_in, w_fb_out, n_fin, l, final_norm=last, tm=tmf_s, tf=tf)
        ckv_s.append(ckv); kr_s.append(kr); sgv_s.append(v_b); pool_s.append(pst); conv_s.append(cst)

    stack_p = lambda xs_, w: jnp.stack(xs_).reshape(depth, n_b, seq, w)
    stack_s = lambda xs_, w: jnp.stack(xs_).reshape(depth, n_db, t_new, w)
    return (xp.reshape(n_b, seq, d), xs.reshape(n_db, t_new, d),
            stack_p(ckv_p, KV_LORA), stack_p(kr_p, MLA_ROPE), mem_k, mem_v, jnp.stack(pool_p), jnp.stack(conv_p),
            stack_s(ckv_s, KV_LORA), stack_s(kr_s, MLA_ROPE), stack_s(sgv_s, GROUP_WIDTH),
            jnp.stack(pool_s), jnp.stack(conv_s))
```

```python
import functools

import jax
import jax.numpy as jnp
from jax import lax
from jax.experimental import pallas as pl
from jax.experimental.pallas import tpu as pltpu

F32 = jnp.float32
BF16 = jnp.bfloat16

GROUP_WIDTH = 512
MLA_HEADS = 4
MLA_NOPE = 128
MLA_ROPE = 64
KV_LORA = 256
Q_LORA = 512
MLA_SCALE = (MLA_NOPE + MLA_ROPE) ** -0.5
ROPE_THETA = 10000.0
SG_HEADS = 4
SG_HEAD_DIM = 128
CHUNK = 128
POOL_WINDOWS = (2, 4, 8, 16)
POOL_GROUP = 128
POOL_STATE = 15
CONV_WIDTH = 31
CONV_STATE = 30
MEM_HEADS = 4
MEM_HEAD_DIM = 128
MEM_WIDTH = 512
EPS = 1e-6
NEG = -1e30

LANES = 128
SUBLANES = 8
V7X_VMEM_BYTES = 64 * 2**20

ROPE_PAD = LANES
COL_B = 0
COL_D = 1024
COL_QC = 2048
COL_C = 2560
COL_CKV = 3072
COL_KR = 3328
COL_KRP = 3456
H_COLS = 3584
QK_WIDTH = KV_LORA + ROPE_PAD
HALO_POOL = 16
HALO_CONV = 32


def _params(semantics, vmem_bytes):
    return pltpu.CompilerParams(dimension_semantics=semantics,
                                vmem_limit_bytes=int(min(vmem_bytes, V7X_VMEM_BYTES - 8 * 2**20)))


def _rms(x, g):
    return x * lax.rsqrt(jnp.mean(x * x, axis=-1, keepdims=True) + EPS) * g


def _ln(x, g, b):
    mu = jnp.mean(x, axis=-1, keepdims=True)
    xc = x - mu
    return xc * lax.rsqrt(jnp.mean(xc * xc, axis=-1, keepdims=True) + EPS) * g + b


def _dot(a, b):
    return jnp.dot(a, b, preferred_element_type=F32)


def _dot_nt(a, b):
    return lax.dot_general(a, b, (((1,), (1,)), ((), ())), preferred_element_type=F32)


def _softmax(s):
    e = jnp.exp(s - jnp.max(s, axis=-1, keepdims=True))
    return e / jnp.sum(e, axis=-1, keepdims=True)


def _ffn_body(x_ref, g_ref, wg_ref, wu_ref, wo_ref, gf_ref, o_ref, xn_ref, *, final_norm):
    j = pl.program_id(1)

    @pl.when(j == 0)
    def _():
        x = x_ref[...]
        xn_ref[...] = _rms(x, g_ref[...]).astype(BF16)
        o_ref[...] = x

    xn = xn_ref[...]
    hg = _dot(xn, wg_ref[...])
    hu = _dot(xn, wu_ref[...])
    act = (hg * jax.nn.sigmoid(hg) * (0.5 * hu)).astype(BF16)
    o_ref[...] += _dot(act, wo_ref[...])

    if final_norm:
        @pl.when(j == pl.num_programs(1) - 1)
        def _():
            o_ref[...] = _rms(o_ref[...], gf_ref[...])


def _ffn(x, gain, w_gate, w_up, w_down, g_final, layer, *, final_norm, tm, tf):
    m, d = x.shape
    f = w_down.shape[0]
    vmem = 2 * 2 * tm * d * 4 + tm * d * 2 + 2 * 3 * d * tf * 2 + tm * d * 4 + 3 * tm * tf * 4
    return pl.pallas_call(
        functools.partial(_ffn_body, final_norm=final_norm),
        out_shape=jax.ShapeDtypeStruct((m, d), F32),
        grid=(m // tm, f // tf),
        in_specs=[
            pl.BlockSpec((tm, d), lambda i, j: (i, 0)),
            pl.BlockSpec((None, 1, d), lambda i, j: (layer, 0, 0)),
            pl.BlockSpec((d, tf), lambda i, j: (0, j)),
            pl.BlockSpec((d, tf), lambda i, j: (0, j)),
            pl.BlockSpec((tf, d), lambda i, j: (j, 0)),
            pl.BlockSpec((1, d), lambda i, j: (0, 0)),
        ],
        out_specs=pl.BlockSpec((tm, d), lambda i, j: (i, 0)),
        scratch_shapes=[pltpu.VMEM((tm, d), BF16)],
        compiler_params=_params(("parallel", "arbitrary"), vmem + 4 * 2**20),
        name="ffn",
    )(x, gain, w_gate, w_up, w_down, g_final)


def _ffn_cast_body(x_ref, g_ref, wg_ref, wu_ref, wo_ref, gf_ref, o_ref, wgb_ref, wub_ref, wob_ref, xn_ref, *,
                   final_norm):
    j = pl.program_id(1)

    @pl.when(j == 0)
    def _():
        x = x_ref[...]
        xn_ref[...] = _rms(x, g_ref[...]).astype(BF16)
        o_ref[...] = x

    wgb_ref[...] = wg_ref[...].astype(BF16)
    wub_ref[...] = wu_ref[...].astype(BF16)
    wob_ref[...] = wo_ref[...].astype(BF16)
    xn = xn_ref[...]
    hg = _dot(xn, wgb_ref[...])
    hu = _dot(xn, wub_ref[...])
    act = (hg * jax.nn.sigmoid(hg) * (0.5 * hu)).astype(BF16)
    o_ref[...] += _dot(act, wob_ref[...])

    if final_norm:
        @pl.when(j == pl.num_programs(1) - 1)
        def _():
            o_ref[...] = _rms(o_ref[...], gf_ref[...])


def _ffn_cast(x, gain, w_in, w_out, g_final, layer, *, final_norm, tf):
    m, d = x.shape
    f = w_out.shape[1]
    nj = f // tf
    vmem = 2 * 2 * m * d * 4 + m * d * 2 + 2 * 3 * d * tf * 6 + 3 * m * tf * 4
    return pl.pallas_call(
        functools.partial(_ffn_cast_body, final_norm=final_norm),
        out_shape=(jax.ShapeDtypeStruct((m, d), F32), jax.ShapeDtypeStruct((d, f), BF16),
                   jax.ShapeDtypeStruct((d, f), BF16), jax.ShapeDtypeStruct((f, d), BF16)),
        grid=(1, nj),
        in_specs=[
            pl.BlockSpec((m, d), lambda i, j: (0, 0)),
            pl.BlockSpec((None, 1, d), lambda i, j: (layer, 0, 0)),
            pl.BlockSpec((None, d, tf), lambda i, j: (layer, 0, j)),
            pl.BlockSpec((None, d, tf), lambda i, j: (layer, 0, j + nj)),
            pl.BlockSpec((None, tf, d), lambda i, j: (layer, j, 0)),
            pl.BlockSpec((1, d), lambda i, j: (0, 0)),
        ],
        out_specs=(pl.BlockSpec((m, d), lambda i, j: (0, 0)), pl.BlockSpec((d, tf), lambda i, j: (0, j)),
                   pl.BlockSpec((d, tf), lambda i, j: (0, j)), pl.BlockSpec((tf, d), lambda i, j: (j, 0))),
        scratch_shapes=[pltpu.VMEM((m, d), BF16)],
        compiler_params=_params(("arbitrary", "arbitrary"), vmem + 4 * 2**20),
        name="ffn_cast",
    )(x, gain, w_in, w_in, w_out, g_final)


def _proj_body(x_ref, g_ref, w_ref, o_ref, xn_ref):
    @pl.when(pl.program_id(1) == 0)
    def _():
        xn_ref[...] = _rms(x_ref[...], g_ref[...]).astype(BF16)

    o_ref[...] = _dot(xn_ref[...], w_ref[...])


def _rms_proj(x, gain, w, layer, *, tm, tn):
    m, d = x.shape
    n = w.shape[2]
    vmem = 2 * tm * d * 4 + tm * d * 2 + 2 * d * tn * 2 + 3 * tm * tn * 4
    return pl.pallas_call(
        _proj_body,
        out_shape=jax.ShapeDtypeStruct((m, n), F32),
        grid=(m // tm, n // tn),
        in_specs=[
            pl.BlockSpec((tm, d), lambda i, j: (i, 0)),
            pl.BlockSpec((None, 1, d), lambda i, j: (layer, 0, 0)),
            pl.BlockSpec((None, d, tn), lambda i, j: (layer, 0, j)),
        ],
        out_specs=pl.BlockSpec((tm, tn), lambda i, j: (i, j)),
        scratch_shapes=[pltpu.VMEM((tm, d), BF16)],
        compiler_params=_params(("parallel", "arbitrary"), vmem + 4 * 2**20),
        name="mix_in_proj",
    )(x, gain, w)


def _memkv_body(x_ref, g_ref, wk_ref, wv_ref, k_ref, v_ref):
    xn = _rms(x_ref[...], g_ref[...]).astype(BF16)
    k_ref[...] = _dot(xn, wk_ref[...])
    v_ref[...] = _dot(xn, wv_ref[...])


def _mem_kv(mem, gain, w_ck, w_cv, *, tm):
    m, d = mem.shape
    depth, _, n = w_ck.shape
    vmem = 2 * tm * d * 4 + tm * d * 2 + 2 * 2 * d * n * 2 + 2 * 2 * tm * n * 4
    out = jax.ShapeDtypeStruct((depth, m, n), F32)
    return pl.pallas_call(
        _memkv_body,
        out_shape=(out, out),
        grid=(depth, m // tm),
        in_specs=[
            pl.BlockSpec((tm, d), lambda l, i: (i, 0)),
            pl.BlockSpec((None, 1, d), lambda l, i: (l, 0, 0)),
            pl.BlockSpec((None, d, n), lambda l, i: (l, 0, 0)),
            pl.BlockSpec((None, d, n), lambda l, i: (l, 0, 0)),
        ],
        out_specs=(pl.BlockSpec((None, tm, n), lambda l, i: (l, i, 0)),
                   pl.BlockSpec((None, tm, n), lambda l, i: (l, i, 0))),
        compiler_params=_params(("parallel", "parallel"), vmem + 4 * 2**20),
        name="mem_kv",
    )(mem, gain, w_ck, w_cv)


def _mla_proj_body(qc_ref, ckv_ref, kr_ref, krp_ref, cos_ref, sin_ref, nq_ref, nkv_ref, wn_ref, wr_ref, wrp_ref,
                   wuk_ref, q_ref, k_ref, ckv_out_ref, kr_out_ref):
    cos = cos_ref[...]
    sin = sin_ref[...]
    qn = _rms(qc_ref[...], nq_ref[...]).astype(BF16)
    q_nope = _dot(qn, wn_ref[...]).astype(BF16)
    q_r = _dot(qn, wr_ref[...])
    q_rp = _dot(qn, wrp_ref[...])
    for h in range(MLA_HEADS):
        q_lat = _dot(q_nope[:, h * MLA_NOPE:(h + 1) * MLA_NOPE], wuk_ref[h])
        q_ref[:, h * QK_WIDTH:h * QK_WIDTH + KV_LORA] = (q_lat * MLA_SCALE).astype(BF16)
        sl = slice(h * ROPE_PAD, (h + 1) * ROPE_PAD)
        rot = q_r[:, sl] * cos + q_rp[:, sl] * sin
        q_ref[:, h * QK_WIDTH + KV_LORA:(h + 1) * QK_WIDTH] = (rot * MLA_SCALE).astype(BF16)
    ckv_n = _rms(ckv_ref[...], nkv_ref[...])
    ckv_out_ref[...] = ckv_n
    k_ref[:, :KV_LORA] = ckv_n.astype(BF16)
    k_rot = kr_ref[...] * cos + krp_ref[...] * sin
    kr_out_ref[...] = k_rot[:, :MLA_ROPE]
    k_ref[:, KV_LORA:] = k_rot.astype(BF16)


def _mla_proj(h, cos, sin, norm_q, norm_kv, w_nope, w_rope, w_ropep, w_ukt, layer, *, tm):
    m = h.shape[0]
    qw = MLA_HEADS * QK_WIDTH
    const = lambda *shape: pl.BlockSpec((None,) + shape, lambda i: (layer,) + (0,) * len(shape))
    vmem = 2 * tm * (Q_LORA + KV_LORA + 4 * ROPE_PAD) * 4 + 2 * tm * (qw + QK_WIDTH) * 2 \
        + 2 * tm * (KV_LORA + LANES) * 4 + 2 * (3 * Q_LORA * 512 + 4 * 128 * 256) * 2 + 8 * tm * 512 * 4
    return pl.pallas_call(
        _mla_proj_body,
        out_shape=(jax.ShapeDtypeStruct((m, qw), BF16), jax.ShapeDtypeStruct((m, QK_WIDTH), BF16),
                   jax.ShapeDtypeStruct((m, KV_LORA), F32), jax.ShapeDtypeStruct((m, MLA_ROPE), F32)),
        grid=(m // tm,),
        in_specs=[
            pl.BlockSpec((tm, Q_LORA), lambda i: (i, COL_QC // Q_LORA)),
            pl.BlockSpec((tm, KV_LORA), lambda i: (i, COL_CKV // KV_LORA)),
            pl.BlockSpec((tm, ROPE_PAD), lambda i: (i, COL_KR // ROPE_PAD)),
            pl.BlockSpec((tm, ROPE_PAD), lambda i: (i, COL_KRP // ROPE_PAD)),
            pl.BlockSpec((tm, ROPE_PAD), lambda i: (i, 0)),
            pl.BlockSpec((tm, ROPE_PAD), lambda i: (i, 0)),
            const(1, Q_LORA), const(1, KV_LORA),
            const(Q_LORA, MLA_HEADS * MLA_NOPE), const(Q_LORA, MLA_HEADS * ROPE_PAD),
            const(Q_LORA, MLA_HEADS * ROPE_PAD), const(MLA_HEADS, MLA_NOPE, KV_LORA),
        ],
        out_specs=(pl.BlockSpec((tm, qw), lambda i: (i, 0)), pl.BlockSpec((tm, QK_WIDTH), lambda i: (i, 0)),
                   pl.BlockSpec((tm, KV_LORA), lambda i: (i, 0)), pl.BlockSpec((tm, MLA_ROPE), lambda i: (i, 0))),
        compiler_params=_params(("parallel",), vmem + 4 * 2**20),
        name="mla_proj",
    )(h, h, h, h, cos, sin, norm_q, norm_kv, w_nope, w_rope, w_ropep, w_ukt)


def _mla_prompt_body(q_ref, k_ref, o_ref, m_ref, l_ref, acc_ref, *, tq):
    qi = pl.program_id(1)
    ki = pl.program_id(2)

    @pl.when(ki == 0)
    def _():
        m_ref[...] = jnp.full(m_ref.shape, NEG, F32)
        l_ref[...] = jnp.zeros(l_ref.shape, F32)
        acc_ref[...] = jnp.zeros(acc_ref.shape, F32)

    @pl.when(ki <= qi)
    def _():
        k = k_ref[...]
        v = k[:, :KV_LORA]
        q_pos = qi * tq + lax.broadcasted_iota(jnp.int32, (tq, tq), 0)
        k_pos = ki * tq + lax.broadcasted_iota(jnp.int32, (tq, tq), 1)
        allowed = k_pos <= q_pos
        for h in range(MLA_HEADS):
            s = _dot_nt(q_ref[:, h * QK_WIDTH:(h + 1) * QK_WIDTH], k)
            s = jnp.where(allowed, s, NEG)
            m_prev = m_ref[h][:, :1]
            l_prev = l_ref[h][:, :1]
            m_new = jnp.maximum(m_prev, jnp.max(s, axis=-1, keepdims=True))
            alpha = jnp.exp(m_prev - m_new)
            p = jnp.exp(s - m_new)
            l_new = alpha * l_prev + jnp.sum(p, axis=-1, keepdims=True)
            acc_ref[h] = alpha * acc_ref[h] + _dot(p.astype(BF16), v)
            m_ref[h] = jnp.broadcast_to(m_new, (tq, LANES))
            l_ref[h] = jnp.broadcast_to(l_new, (tq, LANES))

    @pl.when(ki == qi)
    def _():
        for h in range(MLA_HEADS):
            o_ref[:, h * KV_LORA:(h + 1) * KV_LORA] = (acc_ref[h] / l_ref[h][:, :1]).astype(BF16)


def _mla_prompt(q_full, k_full, n_batch, *, tq):
    m = q_full.shape[0]
    nq = m // n_batch // tq
    qw = MLA_HEADS * QK_WIDTH
    ow = MLA_HEADS * KV_LORA
    vmem = 2 * tq * qw * 2 + 2 * tq * QK_WIDTH * 2 + 2 * tq * ow * 2 \
        + MLA_HEADS * tq * (2 * LANES + KV_LORA) * 4 + 6 * tq * tq * 4
    return pl.pallas_call(
        functools.partial(_mla_prompt_body, tq=tq),
        out_shape=jax.ShapeDtypeStruct((m, ow), BF16),
        grid=(n_batch, nq, nq),
        in_specs=[
            pl.BlockSpec((tq, qw), lambda b, qi, ki: (b * nq + qi, 0)),
            pl.BlockSpec((tq, QK_WIDTH), lambda b, qi, ki: (b * nq + jnp.minimum(ki, qi), 0)),
        ],
        out_specs=pl.BlockSpec((tq, ow), lambda b, qi, ki: (b * nq + qi, 0)),
        scratch_shapes=[pltpu.VMEM((MLA_HEADS, tq, LANES), F32), pltpu.VMEM((MLA_HEADS, tq, LANES), F32),
                        pltpu.VMEM((MLA_HEADS, tq, KV_LORA), F32)],
        compiler_params=_params(("parallel", "parallel", "arbitrary"), vmem + 4 * 2**20),
        name="mla_prompt_attn",
    )(q_full, k_full)


def _ckv_page_copy(ckv_hbm, ckv_buf, sem, layer, page, slot, j, page_size):
    return pltpu.make_async_copy(ckv_hbm.at[layer, page], ckv_buf.at[slot, pl.ds(j * page_size, page_size)],
                                 sem.at[slot])


def _krt_page_copy(krt_hbm, krt_buf, sem, layer, page, slot, j, page_size):
    return pltpu.make_async_copy(krt_hbm.at[layer, page], krt_buf.at[slot, :, pl.ds(j * page_size, page_size)],
                                 sem.at[slot])


def _mla_sample_body(pt_ref, q_ref, kn_ref, ckv_hbm, krt_hbm, o_ref, ckv_buf, krt_buf, kbf_ref, p_ref, onew_ref,
                     sem_c, sem_r, *, layer, n_pages, page_size, chunk, t_new):
    n = pl.program_id(0)
    n_samples = pl.num_programs(0) - 1
    past = n_pages * page_size

    def start_fetch(sample, slot):
        for j in range(n_pages):
            page = pt_ref[sample, j]
            _ckv_page_copy(ckv_hbm, ckv_buf, sem_c, layer, page, slot, j, page_size).start()
            _krt_page_copy(krt_hbm, krt_buf, sem_r, layer, page, slot, j, page_size).start()

    def wait_fetch(slot):
        for j in range(n_pages):
            _ckv_page_copy(ckv_hbm, ckv_buf, sem_c, layer, 0, slot, j, page_size).wait()
            _krt_page_copy(krt_hbm, krt_buf, sem_r, layer, 0, slot, j, page_size).wait()

    slot = lax.rem(n, 2)

    @pl.when(n == 0)
    def _():
        start_fetch(0, 0)
        kbf_ref[1] = jnp.zeros(kbf_ref.shape[1:], BF16)
        p_ref[...] = jnp.zeros(p_ref.shape, BF16)
        onew_ref[...] = jnp.zeros(onew_ref.shape, F32)

    @pl.when(n + 1 < n_samples)
    def _():
        start_fetch(n + 1, 1 - slot)

    @pl.when(n < n_samples)
    def _():
        wait_fetch(slot)

    def step(cur):
        prev = 1 - cur
        o_ref[...] = (_dot(p_ref[...], kbf_ref[prev]) + onew_ref[...]).astype(BF16)

        q = q_ref[...]
        q_lat = q[:, :KV_LORA]
        q_rope = q[:, KV_LORA:KV_LORA + MLA_ROPE]
        rows = q.shape[0]
        for c in range(past // chunk):
            sl = pl.ds(c * chunk, chunk)
            kbf_ref[cur, sl, :] = ckv_buf[cur, sl, :].astype(BF16)
        s_past = _dot_nt(q_lat, kbf_ref[cur]) + _dot(q_rope, krt_buf[cur].astype(BF16))

        kn = kn_ref[...]
        s_new = _dot_nt(q, kn)
        q_tok = lax.broadcasted_iota(jnp.int32, (rows, t_new), 0) // MLA_HEADS
        k_tok = lax.broadcasted_iota(jnp.int32, (rows, t_new), 1)
        s_new = jnp.where(k_tok <= q_tok, s_new, NEG)

        m = jnp.maximum(jnp.max(s_past, axis=-1, keepdims=True), jnp.max(s_new, axis=-1, keepdims=True))
        p_past = jnp.exp(s_past - m)
        p_new = jnp.exp(s_new - m)
        denom = jnp.sum(p_past, axis=-1, keepdims=True) + jnp.sum(p_new, axis=-1, keepdims=True)
        p_ref[...] = (p_past / denom).astype(BF16)
        onew_ref[...] = _dot((p_new / denom).astype(BF16), kn[:, :KV_LORA])

    @pl.when(slot == 0)
    def _():
        step(0)

    @pl.when(slot == 1)
    def _():
        step(1)


def _mla_sample(page_table, q3, kn3, cache_ckv, cache_krope_t, layer, *, chunk):
    n_samples, rows, _ = q3.shape
    t_new = kn3.shape[1]
    n_pages = page_table.shape[1]
    page_size = cache_ckv.shape[2]
    past = n_pages * page_size
    vmem = 2 * past * (KV_LORA + MLA_ROPE) * 4 + past * (2 * KV_LORA + MLA_ROPE) * 2 + rows * past * 4 * 5 \
        + 2 * chunk * KV_LORA * 4
    cur = lambda n, pt: (jnp.minimum(n, n_samples - 1), 0, 0)
    grid_spec = pltpu.PrefetchScalarGridSpec(
        num_scalar_prefetch=1,
        grid=(n_samples + 1,),
        in_specs=[
            pl.BlockSpec((None, rows, QK_WIDTH), cur),
            pl.BlockSpec((None, t_new, QK_WIDTH), cur),
            pl.BlockSpec(memory_space=pl.ANY),
            pl.BlockSpec(memory_space=pl.ANY),
        ],
        out_specs=pl.BlockSpec((None, rows, KV_LORA), lambda n, pt: (jnp.maximum(n - 1, 0), 0, 0)),
        scratch_shapes=[
            pltpu.VMEM((2, past, KV_LORA), F32),
            pltpu.VMEM((2, MLA_ROPE, past), F32),
            pltpu.VMEM((2, past, KV_LORA), BF16),
            pltpu.VMEM((rows, past), BF16),
            pltpu.VMEM((rows, KV_LORA), F32),
            pltpu.SemaphoreType.DMA((2,)),
            pltpu.SemaphoreType.DMA((2,)),
        ],
    )
    return pl.pallas_call(
        functools.partial(_mla_sample_body, layer=layer, n_pages=n_pages, page_size=page_size, chunk=chunk,
                          t_new=t_new),
        out_shape=jax.ShapeDtypeStruct((n_samples, rows, KV_LORA), BF16),
        grid_spec=grid_spec,
        compiler_params=_params(("arbitrary",), vmem + 4 * 2**20),
        name="mla_sample_attn",
    )(page_table, q3, kn3, cache_ckv, cache_krope_t)


def _sg_body(h_ref, w_ref, b_ref, g_ref, beta_ref, y_ref, *v_refs, period, tm):
    row = lax.broadcasted_iota(jnp.int32, (CHUNK, CHUNK), 0)
    col = lax.broadcasted_iota(jnp.int32, (CHUNK, CHUNK), 1)
    shift = period.bit_length() - 1
    keep = (col <= row) & ((row >> shift) == (col >> shift))
    w = [jnp.where(keep, w_ref[h], 0.0).astype(BF16) for h in range(SG_HEADS)]
    bias = b_ref[...]
    for c in range(tm // CHUNK):
        rows = pl.ds(c * CHUNK, CHUNK)
        a = jax.nn.gelu(h_ref[rows, :])
        u = a[:, :GROUP_WIDTH]
        v = _ln(a[:, GROUP_WIDTH:], g_ref[...], beta_ref[...])
        if v_refs:
            v_refs[0][rows, :] = v
        vb = v.astype(BF16)
        for h in range(SG_HEADS):
            sl = slice(h * SG_HEAD_DIM, (h + 1) * SG_HEAD_DIM)
            g = _dot(w[h], vb[:, sl]) + bias[:, sl]
            y_ref[rows, sl] = (u[:, sl] * g).astype(BF16)


def _spatial_gate(h, w, b_rows, ln_g, ln_b, layer, *, period, with_v, tm):
    m = h.shape[0]
    const = lambda *shape: pl.BlockSpec((None,) + shape, lambda i: (layer,) + (0,) * len(shape))
    out_shape = [jax.ShapeDtypeStruct((m, GROUP_WIDTH), BF16)]
    out_specs = [pl.BlockSpec((tm, GROUP_WIDTH), lambda i: (i, 0))]
    if with_v:
        out_shape.append(jax.ShapeDtypeStruct((m, GROUP_WIDTH), F32))
        out_specs.append(pl.BlockSpec((tm, GROUP_WIDTH), lambda i: (i, 0)))
    vmem = 2 * tm * 2 * GROUP_WIDTH * 4 + 2 * tm * GROUP_WIDTH * 6 + 16 * CHUNK * 2 * GROUP_WIDTH * 4
    return pl.pallas_call(
        functools.partial(_sg_body, period=period, tm=tm),
        out_shape=tuple(out_shape),
        grid=(m // tm,),
        in_specs=[
            pl.BlockSpec((tm, 2 * GROUP_WIDTH), lambda i: (i, COL_B // (2 * GROUP_WIDTH))),
            const(SG_HEADS, CHUNK, CHUNK), const(CHUNK, GROUP_WIDTH), const(1, GROUP_WIDTH), const(1, GROUP_WIDTH),
        ],
        out_specs=tuple(out_specs),
        compiler_params=_params(("parallel",), vmem + 4 * 2**20),
        name="spatial_gate",
    )(h, w, b_rows, ln_g, ln_b)


def _conv_tail(y, ln_g, ln_b):
    return jax.nn.silu(_ln(y, ln_g, ln_b))


def _pc_prompt_body(hc_ref, hd_ref, pw_ref, ps_ref, cw_ref, cb_ref, g_ref, beta_ref,
                    yc_ref, yd_ref, pool_out_ref, conv_out_ref, cext, zext, zsh, ybuf, *, tl, rb):
    li = pl.program_id(1)

    @pl.when(li == 0)
    def _():
        cext[0:HALO_POOL, :] = jnp.zeros((HALO_POOL, GROUP_WIDTH), F32)
        zext[0:HALO_CONV, :] = jnp.zeros((HALO_CONV, GROUP_WIDTH), F32)

    hc = hc_ref[...]
    cext[HALO_POOL:HALO_POOL + tl, :] = hc
    pos = li * tl + lax.broadcasted_iota(jnp.int32, (tl, 1), 0)
    for gi, win in enumerate(POOL_WINDOWS):
        sl = slice(gi * POOL_GROUP, (gi + 1) * POOL_GROUP)
        acc = hc[:, sl]
        for k in range(1, win):
            acc = acc + cext[pl.ds(HALO_POOL - k, tl), sl]
        cnt = jnp.minimum(pos + 1, win).astype(F32)
        pooled = acc / cnt - hc[:, sl]
        y = _dot(pooled.astype(BF16), pw_ref[gi]) * ps_ref[:, sl]
        yc_ref[:, sl] = y.astype(BF16)

    hd = hd_ref[...]
    zext[HALO_CONV:HALO_CONV + tl, :] = hd[:, :GROUP_WIDTH] * jax.nn.sigmoid(hd[:, GROUP_WIDTH:])
    first = HALO_CONV - CONV_STATE
    for b in range(SUBLANES):
        span = tl + (CONV_WIDTH - 1 - b) // SUBLANES * SUBLANES
        zsh[b, 0:span, :] = zext[pl.ds(first + b, span), :]
    for r0 in range(0, tl, rb):
        for cg in range(GROUP_WIDTH // LANES):
            sl = slice(cg * LANES, (cg + 1) * LANES)
            acc = jnp.broadcast_to(cb_ref[:, sl], (rb, LANES))
            for k in range(CONV_WIDTH):
                acc = acc + cw_ref[k:k + 1, sl] * zsh[k % SUBLANES, pl.ds(r0 + k // SUBLANES * SUBLANES, rb), sl]
            ybuf[r0:r0 + rb, sl] = acc
    yd_ref[...] = _conv_tail(ybuf[...], g_ref[...], beta_ref[...]).astype(BF16)

    @pl.when(li == pl.num_programs(1) - 1)
    def _():
        pool_out_ref[...] = cext[pl.ds(HALO_POOL + tl - POOL_STATE, POOL_STATE), :]
        conv_out_ref[...] = zext[pl.ds(HALO_CONV + tl - CONV_STATE, CONV_STATE), :]

    cext[0:HALO_POOL, :] = cext[tl:tl + HALO_POOL, :]
    zext[0:HALO_CONV, :] = zext[tl:tl + HALO_CONV, :]


def _pool_conv_prompt(h, n_batch, pool_w, pool_scale, conv_w, conv_b, ln_g, ln_b, layer, *, tl, rb):
    m = h.shape[0]
    nl = m // n_batch // tl
    const = lambda *shape: pl.BlockSpec((None,) + shape, lambda b, l: (layer,) + (0,) * len(shape))
    y = jax.ShapeDtypeStruct((m, GROUP_WIDTH), BF16)
    vmem = 2 * tl * 3 * GROUP_WIDTH * 4 + 2 * 2 * tl * GROUP_WIDTH * 2 + 3 * (tl + HALO_CONV) * GROUP_WIDTH * 4 \
        + 8 * tl * GROUP_WIDTH * 4
    return pl.pallas_call(
        functools.partial(_pc_prompt_body, tl=tl, rb=rb),
        out_shape=(y, y, jax.ShapeDtypeStruct((n_batch, POOL_STATE, GROUP_WIDTH), F32),
                   jax.ShapeDtypeStruct((n_batch, CONV_STATE, GROUP_WIDTH), F32)),
        grid=(n_batch, nl),
        in_specs=[
            pl.BlockSpec((tl, GROUP_WIDTH), lambda b, l: (b * nl + l, COL_C // GROUP_WIDTH)),
            pl.BlockSpec((tl, 2 * GROUP_WIDTH), lambda b, l: (b * nl + l, COL_D // (2 * GROUP_WIDTH))),
            const(len(POOL_WINDOWS), POOL_GROUP, POOL_GROUP), const(1, GROUP_WIDTH),
            const(CONV_WIDTH, GROUP_WIDTH), const(1, GROUP_WIDTH), const(1, GROUP_WIDTH), const(1, GROUP_WIDTH),
        ],
        out_specs=(pl.BlockSpec((tl, GROUP_WIDTH), lambda b, l: (b * nl + l, 0)),
                   pl.BlockSpec((tl, GROUP_WIDTH), lambda b, l: (b * nl + l, 0)),
                   pl.BlockSpec((None, POOL_STATE, GROUP_WIDTH), lambda b, l: (b, 0, 0)),
                   pl.BlockSpec((None, CONV_STATE, GROUP_WIDTH), lambda b, l: (b, 0, 0))),
        scratch_shapes=[pltpu.VMEM((HALO_POOL + tl, GROUP_WIDTH), F32), pltpu.VMEM((HALO_CONV + tl, GROUP_WIDTH), F32),
                        pltpu.VMEM((SUBLANES, tl + HALO_CONV - SUBLANES, GROUP_WIDTH), F32),
                        pltpu.VMEM((tl, GROUP_WIDTH), F32)],
        compiler_params=_params(("parallel", "arbitrary"), vmem + SUBLANES * (tl + HALO_CONV) * GROUP_WIDTH * 4 + 4 * 2**20),
        name="pool_conv_prompt",
    )(h, h, pool_w, pool_scale, conv_w, conv_b, ln_g, ln_b)


def _pc_sample_body(hc_ref, hd_ref, sp_ref, sc_ref, pw_ref, ps_ref, cw_ref, cb_ref, g_ref, beta_ref,
                    yc_ref, yd_ref, pool_out_ref, conv_out_ref, cext, zext, ybuf, *, bs, t_new, past_len):
    rows = bs * t_new
    hc = hc_ref[...]
    cext[:, 0:POOL_STATE, :] = sp_ref[...]
    cext[:, POOL_STATE:POOL_STATE + t_new, :] = hc.reshape(bs, t_new, GROUP_WIDTH)
    pos = past_len + lax.broadcasted_iota(jnp.int32, (1, t_new, 1), 1)
    for gi, win in enumerate(POOL_WINDOWS):
        sl = slice(gi * POOL_GROUP, (gi + 1) * POOL_GROUP)
        acc = cext[:, pl.ds(POOL_STATE, t_new), sl]
        for k in range(1, win):
            acc = acc + cext[:, pl.ds(POOL_STATE - k, t_new), sl]
        cnt = jnp.minimum(pos + 1, win).astype(F32)
        pooled = (acc / cnt).reshape(rows, POOL_GROUP) - hc[:, sl]
        y = _dot(pooled.astype(BF16), pw_ref[gi]) * ps_ref[:, sl]
        yc_ref[:, sl] = y.astype(BF16)
    pool_out_ref[...] = cext[:, pl.ds(t_new, POOL_STATE), :]

    hd = hd_ref[...]
    z = hd[:, :GROUP_WIDTH] * jax.nn.sigmoid(hd[:, GROUP_WIDTH:])
    zext[:, 0:CONV_STATE, :] = sc_ref[...]
    zext[:, CONV_STATE:CONV_STATE + t_new, :] = z.reshape(bs, t_new, GROUP_WIDTH)
    for cg in range(GROUP_WIDTH // LANES):
        sl = slice(cg * LANES, (cg + 1) * LANES)
        acc = jnp.broadcast_to(cb_ref[:, sl].reshape(1, 1, LANES), (bs, t_new, LANES))
        for k in range(CONV_WIDTH):
            acc = acc + cw_ref[k:k + 1, sl].reshape(1, 1, LANES) * zext[:, pl.ds(k, t_new), sl]
        ybuf[:, sl] = acc.reshape(rows, LANES)
    yd_ref[...] = _conv_tail(ybuf[...], g_ref[...], beta_ref[...]).astype(BF16)
    conv_out_ref[...] = zext[:, pl.ds(t_new, CONV_STATE), :]


def _pool_conv_sample(h, state_pool, state_conv, pool_w, pool_scale, conv_w, conv_b, ln_g, ln_b, layer, *,
                      bs, t_new, past_len):
    m = h.shape[0]
    n_samples = m // t_new
    rows = bs * t_new
    const = lambda *shape: pl.BlockSpec((None,) + shape, lambda i: (layer,) + (0,) * len(shape))
    y = jax.ShapeDtypeStruct((m, GROUP_WIDTH), BF16)
    pool_rows = POOL_STATE + t_new + 1
    conv_rows = CONV_STATE + t_new + 2
    vmem = 2 * rows * 3 * GROUP_WIDTH * 4 + 4 * bs * (16 + 32) * GROUP_WIDTH * 4 \
        + bs * (pool_rows + conv_rows) * GROUP_WIDTH * 4 + 8 * rows * GROUP_WIDTH * 4
    return pl.pallas_call(
        functools.partial(_pc_sample_body, bs=bs, t_new=t_new, past_len=past_len),
        out_shape=(y, y, jax.ShapeDtypeStruct((n_samples, POOL_STATE, GROUP_WIDTH), F32),
                   jax.ShapeDtypeStruct((n_samples, CONV_STATE, GROUP_WIDTH), F32)),
        grid=(n_samples // bs,),
        in_specs=[
            pl.BlockSpec((rows, GROUP_WIDTH), lambda i: (i, COL_C // GROUP_WIDTH)),
            pl.BlockSpec((rows, 2 * GROUP_WIDTH), lambda i: (i, COL_D // (2 * GROUP_WIDTH))),
            pl.BlockSpec((None, bs, POOL_STATE, GROUP_WIDTH), lambda i: (layer, i, 0, 0)),
            pl.BlockSpec((None, bs, CONV_STATE, GROUP_WIDTH), lambda i: (layer, i, 0, 0)),
            const(len(POOL_WINDOWS), POOL_GROUP, POOL_GROUP), const(1, GROUP_WIDTH),
            const(CONV_WIDTH, GROUP_WIDTH), const(1, GROUP_WIDTH), const(1, GROUP_WIDTH), const(1, GROUP_WIDTH),
        ],
        out_specs=(pl.BlockSpec((rows, GROUP_WIDTH), lambda i: (i, 0)),
                   pl.BlockSpec((rows, GROUP_WIDTH), lambda i: (i, 0)),
                   pl.BlockSpec((bs, POOL_STATE, GROUP_WIDTH), lambda i: (i, 0, 0)),
                   pl.BlockSpec((bs, CONV_STATE, GROUP_WIDTH), lambda i: (i, 0, 0))),
        scratch_shapes=[pltpu.VMEM((bs, pool_rows, GROUP_WIDTH), F32), pltpu.VMEM((bs, conv_rows, GROUP_WIDTH), F32),
                        pltpu.VMEM((rows, GROUP_WIDTH), F32)],
        compiler_params=_params(("parallel",), vmem + 4 * 2**20),
        name="pool_conv_sample",
    )(h, h, state_pool, state_conv, pool_w, pool_scale, conv_w, conv_b, ln_g, ln_b)


def _mix_out_value(x_ref, ol_ref, yb_ref, yc_ref, yd_ref, wuv_ref, wo_ref):
    ol = ol_ref[...]
    parts = [_dot(ol[:, h * KV_LORA:(h + 1) * KV_LORA], wuv_ref[h]).astype(BF16) for h in range(MLA_HEADS)]
    y = jnp.concatenate(parts + [yb_ref[...], yc_ref[...], yd_ref[...]], axis=1)
    return x_ref[...] + _dot(y, wo_ref[...])


def _mix_out_body(x_ref, ol_ref, yb_ref, yc_ref, yd_ref, wuv_ref, wo_ref, o_ref):
    o_ref[...] = _mix_out_value(x_ref, ol_ref, yb_ref, yc_ref, yd_ref, wuv_ref, wo_ref)


def _mix_out(x, o_lat, y_b, y_c, y_d, w_uv, w_out, layer, *, tm):
    m, d = x.shape
    mw = w_out.shape[1]
    row = lambda w: pl.BlockSpec((tm, w), lambda i: (i, 0))
    vmem = 2 * 2 * tm * d * 4 + 2 * tm * (MLA_HEADS * KV_LORA + 3 * GROUP_WIDTH) * 2 + 2 * mw * d * 2 \
        + tm * mw * 2 + tm * d * 4
    return pl.pallas_call(
        _mix_out_body,
        out_shape=jax.ShapeDtypeStruct((m, d), F32),
        grid=(m // tm,),
        in_specs=[
            row(d), row(MLA_HEADS * KV_LORA), row(GROUP_WIDTH), row(GROUP_WIDTH), row(GROUP_WIDTH),
            pl.BlockSpec((None, MLA_HEADS, KV_LORA, 128), lambda i: (layer, 0, 0, 0)),
            pl.BlockSpec((None, mw, d), lambda i: (layer, 0, 0)),
        ],
        out_specs=row(d),
        compiler_params=_params(("parallel",), vmem + 4 * 2**20),
        name="mix_out_proj",
    )(x, o_lat, y_b, y_c, y_d, w_uv, w_out)


def _cross_prompt_value(x, g_ref, wq_ref, wo_ref, mk_ref, mv_ref):
    xn = _rms(x, g_ref[...]).astype(BF16)
    q = (_dot(xn, wq_ref[...]) * MEM_HEAD_DIM ** -0.5).astype(BF16)
    k = mk_ref[...].astype(BF16)
    v = mv_ref[...].astype(BF16)
    heads = []
    for h in range(MEM_HEADS):
        sl = slice(h * MEM_HEAD_DIM, (h + 1) * MEM_HEAD_DIM)
        p = _softmax(_dot_nt(q[:, sl], k[:, sl]))
        heads.append(_dot(p.astype(BF16), v[:, sl]).astype(BF16))
    return x + _dot(jnp.concatenate(heads, axis=1), wo_ref[...])


def _mix_cross_prompt_body(x_ref, ol_ref, yb_ref, yc_ref, yd_ref, wuv_ref, wout_ref, g_ref, wq_ref, wo_ref,
                           mk_ref, mv_ref, o_ref):
    x1 = _mix_out_value(x_ref, ol_ref, yb_ref, yc_ref, yd_ref, wuv_ref, wout_ref)
    o_ref[...] = _cross_prompt_value(x1, g_ref, wq_ref, wo_ref, mk_ref, mv_ref)


def _mix_cross_prompt(x, o_lat, y_b, y_c, y_d, w_uv, w_out, gain, w_cq, w_co, mem_k, mem_v, layer, *, tq):
    m, d = x.shape
    mw = w_out.shape[1]
    n_batch, mem_len = mem_k.shape[1:3]
    nq = m // n_batch // tq
    row = lambda w: pl.BlockSpec((tq, w), lambda b, i: (b * nq + i, 0))
    const = lambda *shape: pl.BlockSpec((None,) + shape, lambda b, i: (layer,) + (0,) * len(shape))
    mem = pl.BlockSpec((None, None, mem_len, MEM_WIDTH), lambda b, i: (layer, b, 0, 0))
    vmem = 2 * 2 * tq * d * 4 + 2 * tq * (MLA_HEADS * KV_LORA + 3 * GROUP_WIDTH) * 2 + 2 * mw * d * 2 \
        + 2 * 2 * d * MEM_WIDTH * 2 + 2 * 2 * mem_len * MEM_WIDTH * 4 + tq * mw * 2 + tq * d * 10 + 8 * tq * MEM_WIDTH * 4
    return pl.pallas_call(
        _mix_cross_prompt_body,
        out_shape=jax.ShapeDtypeStruct((m, d), F32),
        grid=(n_batch, nq),
        in_specs=[
            row(d), row(MLA_HEADS * KV_LORA), row(GROUP_WIDTH), row(GROUP_WIDTH), row(GROUP_WIDTH),
            const(MLA_HEADS, KV_LORA, 128), const(mw, d),
            const(1, d), const(d, MEM_WIDTH), const(MEM_WIDTH, d), mem, mem,
        ],
        out_specs=row(d),
        compiler_params=_params(("parallel", "parallel"), vmem + 4 * 2**20),
        name="mix_cross_prompt",
    )(x, o_lat, y_b, y_c, y_d, w_uv, w_out, gain, w_cq, w_co, mem_k, mem_v)


def _cross_sample_body(x_ref, g_ref, wq_ref, wo_ref, mk_ref, mv_ref, o_ref, attn_ref, *, bs, t_new):
    x = x_ref[...]
    xn = _rms(x, g_ref[...]).astype(BF16)
    q = _dot(xn, wq_ref[...]) * MEM_HEAD_DIM ** -0.5
    rows = MEM_HEADS * t_new
    lane_head = lax.broadcasted_iota(jnp.int32, (rows, MEM_WIDTH), 1) // MEM_HEAD_DIM
    row_head = lax.broadcasted_iota(jnp.int32, (rows, MEM_WIDTH), 0) // t_new
    own = lane_head == row_head
    scores = []
    for s in range(bs):
        qs = q[s * t_new:(s + 1) * t_new, :]
        q_bd = jnp.where(own, jnp.concatenate([qs] * MEM_HEADS, axis=0), 0.0).astype(BF16)
        scores.append(_dot_nt(q_bd, mk_ref[s].astype(BF16)))
    p = _softmax(jnp.concatenate(scores, axis=0)).astype(BF16)
    for s in range(bs):
        r = jnp.where(own, _dot(p[s * rows:(s + 1) * rows], mv_ref[s].astype(BF16)), 0.0)
        o = r[0:t_new]
        for h in range(1, MEM_HEADS):
            o = o + r[h * t_new:(h + 1) * t_new]
        attn_ref[s * t_new:(s + 1) * t_new, :] = o
    o_ref[...] = x + _dot(attn_ref[...].astype(BF16), wo_ref[...])


def _cross_sample(x, gain, w_cq, w_co, mem_k, mem_v, layer, *, bs, t_new):
    m, d = x.shape
    mem_len = mem_k.shape[2]
    rows = bs * t_new
    const = lambda *shape: pl.BlockSpec((None,) + shape, lambda i: (layer,) + (0,) * len(shape))
    vmem = 2 * 2 * rows * d * 4 + 2 * 2 * d * MEM_WIDTH * 2 + 2 * 2 * bs * mem_len * MEM_WIDTH * 4 \
        + rows * d * 6 + 16 * mem_len * MEM_WIDTH * 4
    return pl.pallas_call(
        functools.partial(_cross_sample_body, bs=bs, t_new=t_new),
        out_shape=jax.ShapeDtypeStruct((m, d), F32),
        grid=(m // rows,),
        in_specs=[
            pl.BlockSpec((rows, d), lambda i: (i, 0)),
            const(1, d), const(d, MEM_WIDTH), const(MEM_WIDTH, d),
            pl.BlockSpec((None, bs, mem_len, MEM_WIDTH), lambda i: (layer, i, 0, 0)),
            pl.BlockSpec((None, bs, mem_len, MEM_WIDTH), lambda i: (layer, i, 0, 0)),
        ],
        out_specs=pl.BlockSpec((rows, d), lambda i: (i, 0)),
        scratch_shapes=[pltpu.VMEM((rows, MEM_WIDTH), F32)],
        compiler_params=_params(("parallel",), vmem + 4 * 2**20),
        name="cross_attn_sample",
    )(x, gain, w_cq, w_co, mem_k, mem_v)


def _rotate_half_cols(w):
    half = w.shape[-1] // 2
    return jnp.concatenate([-w[..., half:], w[..., :half]], axis=-1)


def _pad_cols(w, width):
    return jnp.pad(w, [(0, 0)] * (w.ndim - 1) + [(0, width - w.shape[-1])])


def _rope_tables(pos):
    half = MLA_ROPE // 2
    inv = ROPE_THETA ** (-jnp.arange(half, dtype=F32) / half)
    ang = pos.astype(F32)[:, None] * inv[None, :]
    c = jnp.cos(ang)
    s = jnp.sin(ang)
    return (_pad_cols(jnp.concatenate([c, c], axis=1), ROPE_PAD), _pad_cols(jnp.concatenate([s, s], axis=1), ROPE_PAD))


def _pick_tile(m, pref):
    t = min(m, pref)
    while m % t:
        t //= 2
    return t


def kernel(x_prompt, x_sample, mem_prompt, cache_ckv, cache_krope, cache_mem_k, cache_mem_v, state_pool, state_conv, page_table, w_in, w_out, norm_q_lat, w_qb, norm_kv_lat, w_uk, w_uv, sg_w, sg_b, sg_ln_g, sg_ln_b, pool_w, pool_scale, conv_w, conv_b, conv_ln_g, conv_ln_b, norm_mix, norm_ffn_a, w_ffn_a_in, w_ffn_a_out, norm_ffn_b, w_ffn_b_in, w_ffn_b_out, norm_cross, norm_mem, w_cq, w_ck, w_cv, w_co, norm_final):
    n_b, seq, d = x_prompt.shape
    n_db, t_new, _ = x_sample.shape
    depth = w_in.shape[0]
    n_pages = page_table.shape[1]
    page_size = cache_ckv.shape[2]
    past_len = n_pages * page_size
    mem_len = mem_prompt.shape[1]

    a_cols = Q_LORA + KV_LORA + MLA_ROPE
    b0, c0, d0 = a_cols, a_cols + 2 * GROUP_WIDTH, a_cols + 3 * GROUP_WIDTH
    w_kr = w_in[:, :, Q_LORA + KV_LORA:a_cols]
    w_in_r = jnp.concatenate([
        w_in[:, :, b0:c0], w_in[:, :, d0:], w_in[:, :, :Q_LORA], w_in[:, :, c0:d0],
        w_in[:, :, Q_LORA:Q_LORA + KV_LORA], _pad_cols(w_kr, ROPE_PAD), _pad_cols(_rotate_half_cols(w_kr), ROPE_PAD),
    ], axis=-1).astype(BF16)
    w_q_nope = w_qb[..., :MLA_NOPE].reshape(depth, Q_LORA, MLA_HEADS * MLA_NOPE).astype(BF16)
    w_q_r = w_qb[..., MLA_NOPE:]
    w_q_rope = _pad_cols(w_q_r, ROPE_PAD).reshape(depth, Q_LORA, MLA_HEADS * ROPE_PAD).astype(BF16)
    w_q_ropep = _pad_cols(_rotate_half_cols(w_q_r), ROPE_PAD).reshape(depth, Q_LORA, MLA_HEADS * ROPE_PAD).astype(BF16)
    w_ukt = jnp.transpose(w_uk, (0, 2, 3, 1)).astype(BF16)
    w_uvh = jnp.transpose(w_uv, (0, 2, 1, 3)).astype(BF16)
    w_out_b = w_out.astype(BF16)
    w_cq_b, w_ck_b, w_cv_b, w_co_b = (w.astype(BF16) for w in (w_cq, w_ck, w_cv, w_co))
    pool_w_b = pool_w.astype(BF16)
    row3 = lambda g: g.reshape(depth, 1, g.shape[-1])
    n_mix, n_fa, n_fb, n_cr, n_mem = (row3(g) for g in (norm_mix, norm_ffn_a, norm_ffn_b, norm_cross, norm_mem))
    n_q, n_kv = row3(norm_q_lat), row3(norm_kv_lat)
    sg_g, sg_beta = row3(sg_ln_g), row3(sg_ln_b)
    p_scale, c_b, c_g, c_beta = row3(pool_scale), row3(conv_b), row3(conv_ln_g), row3(conv_ln_b)
    n_fin = norm_final.reshape(1, d)
    reps = CHUNK // t_new
    sg_w_p = sg_w
    sg_w_s = jnp.tile(sg_w[:, :, :t_new, :t_new], (1, 1, reps, reps))
    bias_rows = lambda b: jnp.repeat(jnp.transpose(b, (0, 2, 1)), SG_HEAD_DIM, axis=2)
    sg_b_p = bias_rows(sg_b)
    sg_b_s = jnp.tile(bias_rows(sg_b[:, :, :t_new]), (1, reps, 1))

    cache_krope_t = jnp.swapaxes(cache_krope, 2, 3)

    cos_p, sin_p = _rope_tables(jnp.tile(jnp.arange(seq), n_b))
    cos_s, sin_s = _rope_tables(jnp.tile(past_len + jnp.arange(t_new), n_db))

    mp = n_b * seq
    ms = n_db * t_new
    xp = x_prompt.reshape(mp, d)
    xs = x_sample.reshape(ms, d)

    mem_k, mem_v = _mem_kv(mem_prompt.reshape(n_b * mem_len, d), n_mem, w_ck_b, w_cv_b, tm=_pick_tile(n_b * mem_len, 512))
    mem_k = mem_k.reshape(depth, n_b, mem_len, MEM_WIDTH)
    mem_v = mem_v.reshape(depth, n_b, mem_len, MEM_WIDTH)

    tm_p = _pick_tile(mp, 512)
    tm_s = _pick_tile(ms, 512)
    tmf_p = _pick_tile(mp, 1024)
    tf = 512
    tf_cast = 256
    bs_pc = _pick_tile(n_db, 16)
    bs_cr = _pick_tile(n_db, 8)
    ckv_p, kr_p, pool_p, conv_p = [], [], [], []
    ckv_s, kr_s, sgv_s, pool_s, conv_s = [], [], [], [], []
    for l in range(depth):
        last = l == depth - 1
        xs, wg_b, wu_b, wd_b = _ffn_cast(xs, n_fa, w_ffn_a_in, w_ffn_a_out, n_fin, l, final_norm=False, tf=tf_cast)
        xp = _ffn(xp, n_fa, wg_b, wu_b, wd_b, n_fin, l, final_norm=False, tm=tmf_p, tf=tf)
        h = _rms_proj(xp, n_mix, w_in_r, l, tm=_pick_tile(mp, 1024), tn=1792)
        q_full, k_full, ckv, kr = _mla_proj(h, cos_p, sin_p, n_q, n_kv, w_q_nope, w_q_rope, w_q_ropep, w_ukt, l, tm=tm_p)
        o_lat = _mla_prompt(q_full, k_full, n_b, tq=_pick_tile(seq, 512))
        (y_b,) = _spatial_gate(h, sg_w_p, sg_b_p, sg_g, sg_beta, l, period=CHUNK, with_v=False, tm=tm_p)
        y_c, y_d, pst, cst = _pool_conv_prompt(h, n_b, pool_w_b, p_scale, conv_w, c_b, c_g, c_beta, l,
                                               tl=_pick_tile(seq, 256), rb=128)
        xp = _mix_cross_prompt(xp, o_lat, y_b, y_c, y_d, w_uvh, w_out_b, n_cr, w_cq_b, w_co_b, mem_k, mem_v, l,
                               tq=_pick_tile(seq, 512))
        ckv_p.append(ckv); kr_p.append(kr); pool_p.append(pst); conv_p.append(cst)
        h = _rms_proj(xs, n_mix, w_in_r, l, tm=_pick_tile(ms, 1024), tn=1792)
        q_full, k_full, ckv, kr = _mla_proj(h, cos_s, sin_s, n_q, n_kv, w_q_nope, w_q_rope, w_q_ropep, w_ukt, l, tm=tm_s)
        o_lat = _mla_sample(page_table, q_full.reshape(n_db, t_new * MLA_HEADS, QK_WIDTH),
                            k_full.reshape(n_db, t_new, QK_WIDTH), cache_ckv, cache_krope_t, l,
                            chunk=_pick_tile(past_len, 1024))
        o_lat = o_lat.reshape(ms, MLA_HEADS * KV_LORA)
        y_b, v_b = _spatial_gate(h, sg_w_s, sg_b_s, sg_g, sg_beta, l, period=t_new, with_v=True, tm=tm_s)
        y_c, y_d, pst, cst = _pool_conv_sample(h, state_pool, state_conv, pool_w_b, p_scale, conv_w, c_b, c_g, c_beta,
                                               l, bs=bs_pc, t_new=t_new, past_len=past_len)
        xs = _mix_out(xs, o_lat, y_b, y_c, y_d, w_uvh, w_out_b, l, tm=tm_s)
        xs = _cross_sample(xs, n_cr, w_cq_b, w_co_b, cache_mem_k, cache_mem_v, l, bs=bs_cr, t_new=t_new)
        xs, wg_b, wu_b, wd_b = _ffn_cast(xs, n_fb, w_ffn_b_in, w_ffn_b_out, n_fin, l, final_norm=last, tf=tf_cast)
        xp = _ffn(xp, n_fb, wg_b, wu_b, wd_b, n_fin, l, final_norm=last, tm=tmf_p, tf=tf)
        ckv_s.append(ckv); kr_s.append(kr); sgv_s.append(v_b); pool_s.append(pst); conv_s.append(cst)

    stack_p = lambda xs_, w: jnp.stack(xs_).reshape(depth, n_b, seq, w)
    stack_s = lambda xs_, w: jnp.stack(xs_).reshape(depth, n_db, t_new, w)
    return (xp.reshape(n_b, seq, d), xs.reshape(n_db, t_new, d),
            stack_p(ckv_p, KV_LORA), stack_p(kr_p, MLA_ROPE), mem_k, mem_v, jnp.stack(pool_p), jnp.stack(conv_p),
            stack_s(ckv_s, KV_LORA), stack_s(kr_s, MLA_ROPE), stack_s(sgv_s, GROUP_WIDTH),
            jnp.stack(pool_s), jnp.stack(conv_s))
```
